```python
import jax, jax.numpy as jnp
from jax import lax
import numpy as np

D_MODEL = 1024
BATCH = 4
SEQ = 4096
DEPTH = 1

N_META = 16
GRID_W = 64
NA_HEADS = 8
NA_HEAD_DIM = 64
NA_WIN_H = 8
NA_WIN_W = 16
NA_QBLK_W = 16
NA_KBLK_W = 32
NA_W = NA_HEADS * NA_HEAD_DIM
MLA_HEADS = 8
MLA_NOPE_DIM = 64
MLA_ROPE_DIM = 32
MLA_V_DIM = 64
MLA_Q_RANK = 384
MLA_KV_RANK = 256
MLA_QBLK = 128
MLA_W = MLA_HEADS * MLA_V_DIM
ROPE_THETA = 10000.0
D_FF = 4 * D_MODEL
EPS = 1e-6

IN_SIZES = (NA_W, NA_W, NA_W, MLA_Q_RANK, MLA_KV_RANK, MLA_ROPE_DIM, D_MODEL, D_MODEL)
D_IN = sum(IN_SIZES)
IN_SPLITS = tuple(int(s) for s in np.cumsum(IN_SIZES)[:-1])

kernel_name = "hybrid_na_mla_gated_encoder"


def rmsnorm(x, g):
    xf = x.astype(jnp.float32)
    y = xf * lax.rsqrt(jnp.mean(xf * xf, axis=-1, keepdims=True) + EPS)
    return (y * g.astype(jnp.float32)).astype(x.dtype)


def rope(x, cos, sin):
    half = x.shape[-1] // 2
    x1, x2 = x[..., :half], x[..., half:]
    return jnp.concatenate([x1 * cos - x2 * sin, x2 * cos + x1 * sin], axis=-1).astype(x.dtype)


def neighborhood_attention(q, k, v, rpb):
    B, L, H, dh = q.shape
    n_tok = L - N_META
    rows = n_tok // GRID_W
    kh = min(NA_WIN_H, rows)
    scale = dh ** -0.5
    qm, km, vm = q[:, :N_META], k[:, :N_META], v[:, :N_META]
    qg = q[:, N_META:].reshape(B, rows, GRID_W, H, dh)
    kg = k[:, N_META:].reshape(B, rows, GRID_W, H, dh)
    vg = v[:, N_META:].reshape(B, rows, GRID_W, H, dh)

    s_m = jnp.einsum('bqhd,bkhd->bhqk', qm, km, preferred_element_type=jnp.float32) * scale
    p_m = jax.nn.softmax(s_m, axis=-1).astype(vm.dtype)
    out_meta = jnp.einsum('bhqk,bkhd->bqhd', p_m, vm)

    n_cb = GRID_W // NA_QBLK_W
    qcol = np.arange(GRID_W).reshape(n_cb, NA_QBLK_W)
    cstart = np.clip(qcol - NA_WIN_W // 2, 0, GRID_W - NA_WIN_W)
    kb0 = np.clip(np.arange(n_cb) * NA_QBLK_W - NA_WIN_W // 2, 0, GRID_W - NA_KBLK_W)
    kcol = kb0[:, None] + np.arange(NA_KBLK_W)
    kc = kcol[:, None, :]
    cvalid = (kc >= cstart[..., None]) & (kc < cstart[..., None] + NA_WIN_W)
    dc_idx = np.clip(kc - qcol[:, :, None] + NA_WIN_W - 1, 0, 2 * NA_WIN_W - 2)
    bias_c = rpb.astype(jnp.float32)[:, :, dc_idx]
    cvalid_b = jnp.asarray(cvalid)[None, None, :, :, None, :]

    def row_block(r):
        rs = jnp.clip(r - kh // 2, 0, rows - kh)
        k_rows = lax.dynamic_slice_in_dim(kg, rs, kh, axis=1)
        v_rows = lax.dynamic_slice_in_dim(vg, rs, kh, axis=1)
        k_blk = k_rows[:, :, kcol]
        v_blk = v_rows[:, :, kcol]
        q_row = lax.dynamic_index_in_dim(qg, r, axis=1, keepdims=False)
        q_row = q_row.reshape(B, n_cb, NA_QBLK_W, H, dh)
        s_grid = jnp.einsum('bnqhd,banchd->bhnqac', q_row, k_blk,
                            preferred_element_type=jnp.float32) * scale
        dr_idx = rs + jnp.arange(kh) - r + NA_WIN_H - 1
        bias = bias_c[:, dr_idx].transpose(0, 2, 3, 1, 4)
        s_grid = jnp.where(cvalid_b, s_grid + bias[None], -jnp.inf)
        s_grid = s_grid.reshape(B, H, n_cb, NA_QBLK_W, kh * NA_KBLK_W)
        s_meta = jnp.einsum('bnqhd,bmhd->bhnqm', q_row, km,
                            preferred_element_type=jnp.float32) * scale
        p = jax.nn.softmax(jnp.concatenate([s_meta, s_grid], axis=-1), axis=-1).astype(v.dtype)
        p_meta = p[..., :N_META]
        p_grid = p[..., N_META:].reshape(B, H, n_cb, NA_QBLK_W, kh, NA_KBLK_W)
        out = (jnp.einsum('bhnqm,bmhd->bnqhd', p_meta, vm)
               + jnp.einsum('bhnqac,banchd->bnqhd', p_grid, v_blk))
        return out.reshape(B, GRID_W, H, dh)

    out_grid = lax.map(row_block, jnp.arange(rows))
    out_grid = out_grid.transpose(1, 0, 2, 3, 4).reshape(B, n_tok, H, dh)
    return jnp.concatenate([out_meta, out_grid], axis=1)


def mla_attention(c_q, c_kv, k_rope_raw, q_norm, w_uq, kv_norm, w_ukv, cos, sin):
    B, L, _ = c_q.shape
    H = MLA_HEADS
    q = (rmsnorm(c_q, q_norm) @ w_uq).reshape(B, L, H, MLA_NOPE_DIM + MLA_ROPE_DIM)
    q_nope = q[..., :MLA_NOPE_DIM]
    q_rope = rope(q[..., MLA_NOPE_DIM:], cos[:, None, :], sin[:, None, :])
    kv = (rmsnorm(c_kv, kv_norm) @ w_ukv).reshape(B, L, H, MLA_NOPE_DIM + MLA_V_DIM)
    k_nope, v = kv[..., :MLA_NOPE_DIM], kv[..., MLA_NOPE_DIM:]
    k_rope = rope(k_rope_raw, cos, sin)
    scale = (MLA_NOPE_DIM + MLA_ROPE_DIM) ** -0.5

    def attend(qn, qr):
        s = (jnp.einsum('bqhd,bkhd->bhqk', qn, k_nope, preferred_element_type=jnp.float32)
             + jnp.einsum('bqhr,bkr->bhqk', qr, k_rope, preferred_element_type=jnp.float32)) * scale
        p = jax.nn.softmax(s, axis=-1).astype(v.dtype)
        return jnp.einsum('bhqk,bkhd->bqhd', p, v)

    out_meta = attend(q_nope[:, :N_META], q_rope[:, :N_META])
    n_tok = L - N_META
    n_blk = n_tok // MLA_QBLK
    qn_b = q_nope[:, N_META:].reshape(B, n_blk, MLA_QBLK, H, MLA_NOPE_DIM).swapaxes(0, 1)
    qr_b = q_rope[:, N_META:].reshape(B, n_blk, MLA_QBLK, H, MLA_ROPE_DIM).swapaxes(0, 1)
    out_blk = lax.map(lambda t: attend(t[0], t[1]), (qn_b, qr_b))
    out_blk = out_blk.swapaxes(0, 1).reshape(B, n_tok, H, MLA_V_DIM)
    return jnp.concatenate([out_meta, out_blk], axis=1).reshape(B, L, MLA_W)


def setup_inputs(seed: int = 0) -> dict:
    key = jax.random.key(seed)
    ks = jax.random.split(key, 18)
    f32 = jnp.float32

    def w(k, shape, fan_in):
        return jax.random.normal(k, shape, f32) * (fan_in ** -0.5)

    def gain(k, shape):
        return 1.0 + 0.05 * jax.random.normal(k, shape, f32)

    return {
        "x": jax.random.normal(ks[0], (BATCH, SEQ, D_MODEL), f32),
        "meta": jax.random.normal(ks[1], (N_META, D_MODEL), f32),
        "norm_mix": gain(ks[2], (DEPTH, D_MODEL)),
        "w_in": w(ks[3], (DEPTH, D_MODEL, D_IN), D_MODEL),
        "na_rpb": 0.1 * jax.random.normal(ks[4], (DEPTH, NA_HEADS, 2 * NA_WIN_H - 1, 2 * NA_WIN_W - 1), f32),
        "mla_q_norm": gain(ks[5], (DEPTH, MLA_Q_RANK)),
        "w_uq": w(ks[6], (DEPTH, MLA_Q_RANK, MLA_HEADS * (MLA_NOPE_DIM + MLA_ROPE_DIM)), MLA_Q_RANK),
        "mla_kv_norm": gain(ks[7], (DEPTH, MLA_KV_RANK)),
        "w_ukv": w(ks[8], (DEPTH, MLA_KV_RANK, MLA_HEADS * (MLA_NOPE_DIM + MLA_V_DIM)), MLA_KV_RANK),
        "w_na_out": w(ks[9], (DEPTH, NA_W, D_MODEL), NA_W),
        "w_mla_out": w(ks[10], (DEPTH, MLA_W, D_MODEL), MLA_W),
        "w_out": w(ks[11], (DEPTH, D_MODEL, D_MODEL), D_MODEL),
        "norm_ffn": gain(ks[12], (DEPTH, D_MODEL)),
        "w_ff1": w(ks[13], (DEPTH, D_MODEL, D_FF), D_MODEL),
        "w_ff2": w(ks[14], (DEPTH, D_FF, D_MODEL), D_FF),
        "norm_final": gain(ks[15], (D_MODEL,)),
    }


def reference(x, meta, norm_mix, w_in, na_rpb, mla_q_norm, w_uq, mla_kv_norm, w_ukv,
              w_na_out, w_mla_out, w_out, norm_ffn, w_ff1, w_ff2, norm_final):
    B, S, D = x.shape
    h = jnp.concatenate([jnp.broadcast_to(meta.astype(x.dtype)[None], (B, N_META, D)), x], axis=1)
    L = S + N_META
    pos = jnp.arange(L, dtype=jnp.float32)
    inv_freq = 1.0 / (ROPE_THETA ** (jnp.arange(0, MLA_ROPE_DIM, 2, dtype=jnp.float32) / MLA_ROPE_DIM))
    ang = pos[:, None] * inv_freq[None, :]
    cos, sin = jnp.cos(ang).astype(x.dtype), jnp.sin(ang).astype(x.dtype)

    for l in range(DEPTH):
        hn = rmsnorm(h, norm_mix[l])
        proj = hn @ w_in[l]
        q_na, k_na, v_na, c_q, c_kv, k_rope_raw, g_na, g_mla = jnp.split(proj, IN_SPLITS, axis=-1)
        o_na = neighborhood_attention(
            q_na.reshape(B, L, NA_HEADS, NA_HEAD_DIM),
            k_na.reshape(B, L, NA_HEADS, NA_HEAD_DIM),
            v_na.reshape(B, L, NA_HEADS, NA_HEAD_DIM),
            na_rpb[l]).reshape(B, L, NA_W) @ w_na_out[l]
        o_mla = mla_attention(c_q, c_kv, k_rope_raw, mla_q_norm[l], w_uq[l],
                              mla_kv_norm[l], w_ukv[l], cos, sin) @ w_mla_out[l]
        merged = jax.nn.sigmoid(g_na) * o_na + jax.nn.sigmoid(g_mla) * o_mla
        h = h + merged @ w_out[l]
        fn = rmsnorm(h, norm_ffn[l])
        h = h + jnp.square(jax.nn.relu(fn @ w_ff1[l])) @ w_ff2[l]

    return rmsnorm(h, norm_final)[:, N_META:]
```

```python
import functools

import numpy as np
import jax
import jax.numpy as jnp
from jax import lax
from jax.experimental import pallas as pl
from jax.experimental.pallas import tpu as pltpu

D_MODEL = 1024
N_META = 16
GRID_W = 64
NA_HEADS = 8
NA_HEAD_DIM = 64
NA_WIN_H = 8
NA_WIN_W = 16
NA_W = NA_HEADS * NA_HEAD_DIM
MLA_HEADS = 8
MLA_NOPE_DIM = 64
MLA_ROPE_DIM = 32
MLA_V_DIM = 64
MLA_Q_RANK = 384
MLA_KV_RANK = 256
ROPE_THETA = 10000.0
D_FF = 4 * D_MODEL
EPS = 1e-6

LANES = 128
HEAD_GROUP = 128
ROPE_LANE0 = MLA_NOPE_DIM
MLA_QK = MLA_HEADS * HEAD_GROUP
NEG = -1e30

C_QKV = 0
C_CQ = 3 * NA_W
C_CKV = C_CQ + MLA_Q_RANK
C_KR = C_CKV + MLA_KV_RANK
C_G = C_KR + HEAD_GROUP
D_IN_EXT = C_G + 2 * D_MODEL

NA_ROWS_PER_BLOCK = 4
NA_KEY_ROWS = NA_ROWS_PER_BLOCK + NA_WIN_H
NA_Q = NA_ROWS_PER_BLOCK * GRID_W
NA_K = NA_KEY_ROWS * GRID_W
META_PAD = LANES

VMEM_LIMIT = 56 * 1024 * 1024

_NT = (((1,), (1,)), ((), ()))


def _const_spec(shape):
    nd = len(shape)
    return pl.BlockSpec(shape, lambda *_: (0,) * nd, pipeline_mode=pl.Buffered(1))


def _rms(x, g):
    return x * lax.rsqrt(jnp.mean(x * x, axis=-1, keepdims=True) + EPS) * g


def _rope_group(x, cos, sin):
    lane = lax.broadcasted_iota(jnp.int32, x.shape, 1)
    partner = jnp.where(lane < ROPE_LANE0 + MLA_ROPE_DIM // 2,
                        pltpu.roll(x, HEAD_GROUP - MLA_ROPE_DIM // 2, 1),
                        pltpu.roll(x, MLA_ROPE_DIM // 2, 1))
    return x * cos + partner * sin


def _inproj_kernel(x_ref, nmix_ref, win_ref, qn_ref, wuq_ref, kvn_ref, wuk_ref, wuv_ref,
                   cosq_ref, sinq_ref, cosk_ref, sink_ref, vone_ref,
                   qkv_ref, qm_ref, km_ref, vm_ref, g_ref):
    bf = jnp.bfloat16
    f32 = jnp.float32
    hn = _rms(x_ref[...], nmix_ref[...]).astype(bf)

    qkv = jnp.dot(hn, win_ref[:, C_QKV:C_CQ], preferred_element_type=f32)
    qkv_ref[:, :NA_W] = (qkv[:, :NA_W] * (NA_HEAD_DIM ** -0.5)).astype(bf)
    qkv_ref[:, NA_W:] = qkv[:, NA_W:].astype(bf)

    g_ref[...] = jnp.dot(hn, win_ref[:, C_G:], preferred_element_type=f32).astype(bf)

    lat = jnp.dot(hn, win_ref[:, C_CQ:C_G], preferred_element_type=f32)
    cq = _rms(lat[:, :MLA_Q_RANK], qn_ref[...]).astype(bf)
    ckv = _rms(lat[:, MLA_Q_RANK:MLA_Q_RANK + MLA_KV_RANK], kvn_ref[...]).astype(bf)
    kr = lat[:, MLA_Q_RANK + MLA_KV_RANK:]
    kr = _rope_group(kr, cosk_ref[...], sink_ref[...])

    q = jnp.dot(cq, wuq_ref[...], preferred_element_type=f32)
    kn = jnp.dot(ckv, wuk_ref[...], preferred_element_type=f32)
    vv = jnp.dot(ckv, wuv_ref[...], preferred_element_type=f32)
    cosq = cosq_ref[...]
    sinq = sinq_ref[...]
    vone = vone_ref[...]
    for h in range(MLA_HEADS):
        sl = slice(h * HEAD_GROUP, (h + 1) * HEAD_GROUP)
        qm_ref[:, sl] = _rope_group(q[:, sl], cosq, sinq).astype(bf)
        km_ref[:, sl] = (kn[:, sl] + kr).astype(bf)
        vm_ref[:, sl] = (vv[:, sl] + vone).astype(bf)


def _inproj(x2, nmix, win, qn, wuq, kvn, wuk, wuv, cosq, sinq, cosk, sink, vone, tm):
    t = x2.shape[0]
    n_tab = cosq.shape[0] // tm
    row = lambda i: (i, 0)
    tab = lambda i: (i % n_tab, 0)
    bf = jnp.bfloat16
    return pl.pallas_call(
        _inproj_kernel,
        grid=(t // tm,),
        in_specs=[
            pl.BlockSpec((tm, D_MODEL), row),
            _const_spec((1, D_MODEL)),
            _const_spec((D_MODEL, D_IN_EXT)),
            _const_spec((1, MLA_Q_RANK)),
            _const_spec((MLA_Q_RANK, MLA_QK)),
            _const_spec((1, MLA_KV_RANK)),
            _const_spec((MLA_KV_RANK, MLA_QK)),
            _const_spec((MLA_KV_RANK, MLA_QK)),
            pl.BlockSpec((tm, HEAD_GROUP), tab),
            pl.BlockSpec((tm, HEAD_GROUP), tab),
            pl.BlockSpec((tm, HEAD_GROUP), tab),
            pl.BlockSpec((tm, HEAD_GROUP), tab),
            _const_spec((1, HEAD_GROUP)),
        ],
        out_specs=[
            pl.BlockSpec((tm, 3 * NA_W), row),
            pl.BlockSpec((tm, MLA_QK), row),
            pl.BlockSpec((tm, MLA_QK), row),
            pl.BlockSpec((tm, MLA_QK), row),
            pl.BlockSpec((tm, 2 * D_MODEL), row),
        ],
        out_shape=[
            jax.ShapeDtypeStruct((t, 3 * NA_W), bf),
            jax.ShapeDtypeStruct((t, MLA_QK), bf),
            jax.ShapeDtypeStruct((t, MLA_QK), bf),
            jax.ShapeDtypeStruct((t, MLA_QK), bf),
            jax.ShapeDtypeStruct((t, 2 * D_MODEL), bf),
        ],
        compiler_params=pltpu.CompilerParams(
            dimension_semantics=("parallel",), vmem_limit_bytes=VMEM_LIMIT),
        name="inproj",
    )(x2, nmix, win, qn, wuq, kvn, wuk, wuv, cosq, sinq, cosk, sink, vone)


def _na_kernel(q_ref, k_ref, v_ref, km_ref, vm_ref, tbl_ref, o_ref, *, n_blocks):
    bf = jnp.bfloat16
    f32 = jnp.float32
    j = pl.program_id(2)
    rows = n_blocks * NA_ROWS_PER_BLOCK
    ws = jnp.clip(j * NA_ROWS_PER_BLOCK - NA_WIN_H // 2, 0, rows - NA_KEY_ROWS)
    start = pl.multiple_of(ws * GRID_W, GRID_W)
    variant = jnp.where(j == 0, 0, jnp.where(j == n_blocks - 1, 2, 1))
    kb = k_ref[pl.ds(start, NA_K), :]
    vb = v_ref[pl.ds(start, NA_K), :]
    kmeta = km_ref[...]
    vmeta = vm_ref[...]
    q2 = q_ref[...]
    lane = lax.broadcasted_iota(jnp.int32, q2.shape, 1)
    outs = []
    for hh in range(2):
        in_head = (lane >= hh * NA_HEAD_DIM) & (lane < (hh + 1) * NA_HEAD_DIM)
        qh = jnp.where(in_head, q2, jnp.zeros_like(q2))
        s_m = lax.dot_general(qh, kmeta, _NT, preferred_element_type=f32) + tbl_ref[variant, hh, :, :META_PAD]
        s_g = lax.dot_general(qh, kb, _NT, preferred_element_type=f32) + tbl_ref[variant, hh, :, META_PAD:]
        m = jnp.maximum(jnp.max(s_m, axis=1, keepdims=True), jnp.max(s_g, axis=1, keepdims=True))
        p_m = jnp.exp(s_m - m)
        p_g = jnp.exp(s_g - m)
        l = jnp.sum(p_m, axis=1, keepdims=True) + jnp.sum(p_g, axis=1, keepdims=True)
        o = (jnp.dot(p_m.astype(bf), vmeta, preferred_element_type=f32)
             + jnp.dot(p_g.astype(bf), vb, preferred_element_type=f32))
        outs.append(o / l)
    o_ref[...] = jnp.where(lane < NA_HEAD_DIM, outs[0], outs[1]).astype(bf)


def _na_attention(qkv, qkv_meta, tbl, batch, seq):
    n_blocks = seq // NA_Q
    n_pairs = NA_HEADS // 2
    kcol = NA_W // LANES
    vcol = 2 * NA_W // LANES
    return pl.pallas_call(
        functools.partial(_na_kernel, n_blocks=n_blocks),
        grid=(n_pairs, batch, n_blocks),
        in_specs=[
            pl.BlockSpec((NA_Q, LANES), lambda hp, b, j: (b * n_blocks + j, hp)),
            pl.BlockSpec((seq, LANES), lambda hp, b, j: (b, kcol + hp)),
            pl.BlockSpec((seq, LANES), lambda hp, b, j: (b, vcol + hp)),
            pl.BlockSpec((META_PAD, LANES), lambda hp, b, j: (0, kcol + hp)),
            pl.BlockSpec((META_PAD, LANES), lambda hp, b, j: (0, vcol + hp)),
            pl.BlockSpec((3, 2, NA_Q, META_PAD + NA_K), lambda hp, b, j: (0, hp, 0, 0)),
        ],
        out_specs=pl.BlockSpec((NA_Q, LANES), lambda hp, b, j: (b * n_blocks + j, hp)),
        out_shape=jax.ShapeDtypeStruct((batch * seq, NA_W), jnp.bfloat16),
        compiler_params=pltpu.CompilerParams(
            dimension_semantics=("parallel", "parallel", "arbitrary"), vmem_limit_bytes=VMEM_LIMIT),
        name="na_attn",
    )(qkv, qkv, qkv, qkv_meta, qkv_meta, tbl)


def _na_bias_table(rpb, seq):
    rows = seq // GRID_W
    n_blocks = rows // NA_ROWS_PER_BLOCK
    kh = min(NA_WIN_H, rows)
    rq = np.arange(NA_ROWS_PER_BLOCK)[:, None, None, None]
    qc = np.arange(GRID_W)[None, :, None, None]
    rk = np.arange(NA_KEY_ROWS)[None, None, :, None]
    kc = np.arange(GRID_W)[None, None, None, :]
    cstart = np.clip(qc - NA_WIN_W // 2, 0, GRID_W - NA_WIN_W)
    cvalid = (kc >= cstart) & (kc < cstart + NA_WIN_W)
    dc = np.clip(kc - qc + NA_WIN_W - 1, 0, 2 * NA_WIN_W - 2)
    valid, dr_all = [], []
    for j in (0, 1, n_blocks - 1):
        r0 = j * NA_ROWS_PER_BLOCK
        ws = int(np.clip(r0 - NA_WIN_H // 2, 0, rows - NA_KEY_ROWS))
        r = r0 + rq
        krow = ws + rk
        rs = np.clip(r - kh // 2, 0, rows - kh)
        rvalid = (krow >= rs) & (krow < rs + kh)
        dr = np.clip(krow - r + NA_WIN_H - 1, 0, 2 * NA_WIN_H - 2)
        full = (NA_ROWS_PER_BLOCK, GRID_W, NA_KEY_ROWS, GRID_W)
        valid.append(np.broadcast_to(rvalid & cvalid, full).reshape(NA_Q, NA_K))
        dr_all.append(np.broadcast_to(dr, full).reshape(NA_Q, NA_K))
    valid = np.stack(valid)
    dr_all = np.stack(dr_all)
    dc_all = np.broadcast_to(dc, (3, NA_ROWS_PER_BLOCK, GRID_W, NA_KEY_ROWS, GRID_W)).reshape(3, NA_Q, NA_K)
    bias = rpb.astype(jnp.float32)[:, dr_all, dc_all]
    grid_part = jnp.where(jnp.asarray(valid)[None], bias, NEG).transpose(1, 0, 2, 3)
    meta_part = np.where(np.arange(META_PAD) < N_META, 0.0, NEG).astype(np.float32)
    meta_part = jnp.broadcast_to(jnp.asarray(meta_part), (3, NA_HEADS, NA_Q, META_PAD))
    return jnp.concatenate([meta_part, grid_part], axis=-1)


def _mla_kernel(q_ref, k_ref, v_ref, km_ref, vm_ref, o_ref, *, tk):
    bf = jnp.bfloat16
    f32 = jnp.float32
    tq = q_ref.shape[0]
    n_chunks = k_ref.shape[0] // tk
    meta_lane = lax.broadcasted_iota(jnp.int32, (tq, META_PAD), 1)
    outs = []
    for hh in range(2):
        sl = slice(hh * HEAD_GROUP, (hh + 1) * HEAD_GROUP)
        qh = q_ref[:, sl]
        s0 = lax.dot_general(qh, km_ref[:, sl], _NT, preferred_element_type=f32)
        s0 = jnp.where(meta_lane < N_META, s0, NEG)
        m0 = jnp.max(s0, axis=1, keepdims=True)
        p0 = jnp.exp(s0 - m0)
        acc0 = jnp.dot(p0.astype(bf), vm_ref[:, sl], preferred_element_type=f32)

        def body(c, carry):
            m, acc = carry
            off = pl.multiple_of(c * tk, tk)
            s = lax.dot_general(qh, k_ref[pl.ds(off, tk), sl], _NT, preferred_element_type=f32)
            m_new = jnp.maximum(m, jnp.max(s, axis=1, keepdims=True))
            alpha = jnp.exp(m - m_new)
            p = jnp.exp(s - m_new)
            acc = acc * alpha + jnp.dot(p.astype(bf), v_ref[pl.ds(off, tk), sl], preferred_element_type=f32)
            return m_new, acc

        _, acc = lax.fori_loop(0, n_chunks, body, (m0, acc0))
        outs.append(acc / acc[:, MLA_V_DIM:MLA_V_DIM + 1])
    lane = lax.broadcasted_iota(jnp.int32, (tq, HEAD_GROUP), 1)
    o_ref[...] = jnp.where(lane < MLA_V_DIM, outs[0], pltpu.roll(outs[1], MLA_V_DIM, 1)).astype(bf)


def _mla_attention(qm, km, vm, km_meta, vm_meta, batch, seq, tq, tk):
    nq = seq // tq
    n_pairs = MLA_HEADS // 2
    pair = 2 * HEAD_GROUP
    return pl.pallas_call(
        functools.partial(_mla_kernel, tk=tk),
        grid=(batch, n_pairs, nq),
        in_specs=[
            pl.BlockSpec((tq, pair), lambda b, hp, i: (b * nq + i, hp)),
            pl.BlockSpec((seq, pair), lambda b, hp, i: (b, hp)),
            pl.BlockSpec((seq, pair), lambda b, hp, i: (b, hp)),
            pl.BlockSpec((META_PAD, pair), lambda b, hp, i: (0, hp)),
            pl.BlockSpec((META_PAD, pair), lambda b, hp, i: (0, hp)),
        ],
        out_specs=pl.BlockSpec((tq, 2 * MLA_V_DIM), lambda b, hp, i: (b * nq + i, hp)),
        out_shape=jax.ShapeDtypeStruct((batch * seq, MLA_HEADS * MLA_V_DIM), jnp.bfloat16),
        compiler_params=pltpu.CompilerParams(
            dimension_semantics=("parallel", "parallel", "arbitrary"), vmem_limit_bytes=VMEM_LIMIT),
        name="mla_attn",
    )(qm, km, vm, km_meta, vm_meta)


def _post_kernel(x_ref, ona_ref, omla_ref, g_ref, wna_ref, wmla_ref, wout_ref, nffn_ref,
                 wff1_ref, wff2_ref, nfin_ref, o_ref):
    bf = jnp.bfloat16
    f32 = jnp.float32
    a = jnp.dot(ona_ref[...], wna_ref[...], preferred_element_type=f32)
    b = jnp.dot(omla_ref[...], wmla_ref[...], preferred_element_type=f32)
    g_na = g_ref[:, :D_MODEL].astype(f32)
    g_mla = g_ref[:, D_MODEL:].astype(f32)
    merged = jax.nn.sigmoid(g_na) * a + jax.nn.sigmoid(g_mla) * b
    h = x_ref[...] + jnp.dot(merged.astype(bf), wout_ref[...], preferred_element_type=f32)
    fn = _rms(h, nffn_ref[...]).astype(bf)
    u = jnp.dot(fn, wff1_ref[...], preferred_element_type=f32)
    u = jnp.square(jnp.maximum(u, 0.0)).astype(bf)
    h = h + jnp.dot(u, wff2_ref[...], preferred_element_type=f32)
    o_ref[...] = _rms(h, nfin_ref[...])


def _post(x2, o_na, o_mla, g, wna, wmla, wout, nffn, wff1, wff2, nfin, tm):
    t = x2.shape[0]
    row = lambda i: (i, 0)
    return pl.pallas_call(
        _post_kernel,
        grid=(t // tm,),
        in_specs=[
            pl.BlockSpec((tm, D_MODEL), row),
            pl.BlockSpec((tm, NA_W), row),
            pl.BlockSpec((tm, MLA_HEADS * MLA_V_DIM), row),
            pl.BlockSpec((tm, 2 * D_MODEL), row),
            _const_spec((NA_W, D_MODEL)),
            _const_spec((MLA_HEADS * MLA_V_DIM, D_MODEL)),
            _const_spec((D_MODEL, D_MODEL)),
            _const_spec((1, D_MODEL)),
            _const_spec((D_MODEL, D_FF)),
            _const_spec((D_FF, D_MODEL)),
            _const_spec((1, D_MODEL)),
        ],
        out_specs=pl.BlockSpec((tm, D_MODEL), row),
        out_shape=jax.ShapeDtypeStruct((t, D_MODEL), jnp.float32),
        compiler_params=pltpu.CompilerParams(
            dimension_semantics=("parallel",), vmem_limit_bytes=VMEM_LIMIT),
        name="post",
    )(x2, o_na, o_mla, g, wna, wmla, wout, nffn, wff1, wff2, nfin)


def _head_groups(w, per_head, lane0):
    k = w.shape[0]
    w = w.reshape(k, MLA_HEADS, per_head)
    w = jnp.pad(w, ((0, 0), (0, 0), (lane0, HEAD_GROUP - lane0 - per_head)))
    return w.reshape(k, MLA_HEADS * HEAD_GROUP)


def _rope_tables(pos, scale):
    half = MLA_ROPE_DIM // 2
    inv_freq = 1.0 / (ROPE_THETA ** (jnp.arange(0, MLA_ROPE_DIM, 2, dtype=jnp.float32) / MLA_ROPE_DIM))
    ang = pos[:, None] * inv_freq[None, :]
    cos, sin = jnp.cos(ang), jnp.sin(ang)
    n = pos.shape[0]
    ones = jnp.ones((n, ROPE_LANE0), jnp.float32)
    tail = HEAD_GROUP - ROPE_LANE0 - MLA_ROPE_DIM
    cos_t = jnp.concatenate([ones, cos, cos, jnp.ones((n, tail), jnp.float32)], axis=1) * scale
    sin_t = jnp.concatenate([0 * ones, -sin, sin, jnp.zeros((n, tail), jnp.float32)], axis=1) * scale
    return cos_t, sin_t


def kernel(x, meta, norm_mix, w_in, na_rpb, mla_q_norm, w_uq, mla_kv_norm, w_ukv, w_na_out, w_mla_out,
           w_out, norm_ffn, w_ff1, w_ff2, norm_final):
    assert norm_mix.shape[0] == 1, "single-layer block"
    batch, seq, d = x.shape
    bf = jnp.bfloat16
    f32 = jnp.float32
    x2 = x.reshape(batch * seq, d)

    wi = w_in[0]
    kr_cols = jnp.pad(wi[:, C_KR:C_KR + MLA_ROPE_DIM],
                      ((0, 0), (ROPE_LANE0, HEAD_GROUP - ROPE_LANE0 - MLA_ROPE_DIM)))
    win = jnp.concatenate([wi[:, :C_KR], kr_cols, wi[:, C_KR + MLA_ROPE_DIM:]], axis=1).astype(bf)
    wuq = _head_groups(w_uq[0], MLA_NOPE_DIM + MLA_ROPE_DIM, 0).astype(bf)
    wukv = w_ukv[0].reshape(MLA_KV_RANK, MLA_HEADS, MLA_NOPE_DIM + MLA_V_DIM)
    wuk = _head_groups(wukv[:, :, :MLA_NOPE_DIM].reshape(MLA_KV_RANK, -1), MLA_NOPE_DIM, 0).astype(bf)
    wuv = _head_groups(wukv[:, :, MLA_NOPE_DIM:].reshape(MLA_KV_RANK, -1), MLA_V_DIM, 0).astype(bf)
    vone = (jnp.arange(HEAD_GROUP) == MLA_V_DIM).astype(f32)[None]

    mla_scale = (MLA_NOPE_DIM + MLA_ROPE_DIM) ** -0.5
    pos_tok = jnp.arange(N_META, N_META + seq, dtype=f32)
    pos_meta = jnp.arange(N_META, dtype=f32)
    cosq, sinq = _rope_tables(pos_tok, mla_scale)
    cosk, sink = _rope_tables(pos_tok, 1.0)
    cosq_m, sinq_m = _rope_tables(pos_meta, mla_scale)
    cosk_m, sink_m = _rope_tables(pos_meta, 1.0)

    small = (norm_mix[0][None], win, mla_q_norm[0][None], wuq, mla_kv_norm[0][None], wuk, wuv)
    qkv, qm, km, vm, g = _inproj(x2, *small, cosq, sinq, cosk, sink, vone, tm=512)
    qkv_meta, _, km_meta, vm_meta, _ = _inproj(meta.astype(f32), *small, cosq_m, sinq_m, cosk_m, sink_m, vone,
                                               tm=N_META)
    pad_rows = ((0, META_PAD - N_META), (0, 0))
    qkv_meta = jnp.pad(qkv_meta, pad_rows)
    km_meta = jnp.pad(km_meta, pad_rows)
    vm_meta = jnp.pad(vm_meta, pad_rows)

    tbl = _na_bias_table(na_rpb[0], seq)
    o_na = _na_attention(qkv, qkv_meta, tbl, batch, seq)
    o_mla = _mla_attention(qm, km, vm, km_meta, vm_meta, batch, seq, tq=256, tk=512)

    out = _post(x2, o_na, o_mla, g, w_na_out[0].astype(bf), w_mla_out[0].astype(bf), w_out[0].astype(bf),
                norm_ffn[0][None], w_ff1[0].astype(bf), w_ff2[0].astype(bf), norm_final[None], tm=512)
    return out.reshape(batch, seq, d)
```

```python
import functools

import numpy as np
import jax
import jax.numpy as jnp
from jax import lax
from jax.experimental import pallas as pl
from jax.experimental.pallas import tpu as pltpu

D_MODEL = 1024
N_META = 16
GRID_W = 64
NA_HEADS = 8
NA_HEAD_DIM = 64
NA_WIN_H = 8
NA_WIN_W = 16
NA_W = NA_HEADS * NA_HEAD_DIM
MLA_HEADS = 8
MLA_NOPE_DIM = 64
MLA_ROPE_DIM = 32
MLA_V_DIM = 64
MLA_Q_RANK = 384
MLA_KV_RANK = 256
ROPE_THETA = 10000.0
D_FF = 4 * D_MODEL
EPS = 1e-6

LANES = 128
HEAD_GROUP = 128
ROPE_LANE0 = MLA_NOPE_DIM
MLA_QK = MLA_HEADS * HEAD_GROUP
NEG = -1e30

C_QKV = 0
C_CQ = 3 * NA_W
C_CKV = C_CQ + MLA_Q_RANK
C_KR = C_CKV + MLA_KV_RANK
C_G = C_KR + HEAD_GROUP
D_IN_EXT = C_G + 2 * D_MODEL

NA_ROWS_PER_BLOCK = 4
NA_KEY_ROWS = NA_ROWS_PER_BLOCK + NA_WIN_H
NA_Q = NA_ROWS_PER_BLOCK * GRID_W
NA_K = NA_KEY_ROWS * GRID_W
META_PAD = LANES

VMEM_LIMIT = 56 * 1024 * 1024

_NT = (((1,), (1,)), ((), ()))


def _const_spec(shape):
    nd = len(shape)
    return pl.BlockSpec(shape, lambda *_: (0,) * nd, pipeline_mode=pl.Buffered(1))


def _rms(x, g):
    return x * lax.rsqrt(jnp.mean(x * x, axis=-1, keepdims=True) + EPS) * g


def _rope_group(x, cos, sin):
    lane = lax.broadcasted_iota(jnp.int32, x.shape, 1)
    partner = jnp.where(lane < ROPE_LANE0 + MLA_ROPE_DIM // 2,
                        pltpu.roll(x, HEAD_GROUP - MLA_ROPE_DIM // 2, 1),
                        pltpu.roll(x, MLA_ROPE_DIM // 2, 1))
    return x * cos + partner * sin


def _inproj_kernel(x_ref, nmix_ref, win_ref, qn_ref, wuq_ref, kvn_ref, wuk_ref, wuv_ref,
                   cosq_ref, sinq_ref, cosk_ref, sink_ref, vone_ref,
                   qkv_ref, qm_ref, km_ref, vm_ref, g_ref):
    bf = jnp.bfloat16
    f32 = jnp.float32
    hn = _rms(x_ref[...], nmix_ref[...]).astype(bf)

    qkv = jnp.dot(hn, win_ref[:, C_QKV:C_CQ], preferred_element_type=f32)
    qkv_ref[:, :NA_W] = (qkv[:, :NA_W] * (NA_HEAD_DIM ** -0.5)).astype(bf)
    qkv_ref[:, NA_W:] = qkv[:, NA_W:].astype(bf)

    g_ref[...] = jnp.dot(hn, win_ref[:, C_G:], preferred_element_type=f32).astype(bf)

    lat = jnp.dot(hn, win_ref[:, C_CQ:C_G], preferred_element_type=f32)
    cq = _rms(lat[:, :MLA_Q_RANK], qn_ref[...]).astype(bf)
    ckv = _rms(lat[:, MLA_Q_RANK:MLA_Q_RANK + MLA_KV_RANK], kvn_ref[...]).astype(bf)
    kr = lat[:, MLA_Q_RANK + MLA_KV_RANK:]
    kr = _rope_group(kr, cosk_ref[...], sink_ref[...])

    q = jnp.dot(cq, wuq_ref[...], preferred_element_type=f32)
    kn = jnp.dot(ckv, wuk_ref[...], preferred_element_type=f32)
    vv = jnp.dot(ckv, wuv_ref[...], preferred_element_type=f32)
    cosq = cosq_ref[...]
    sinq = sinq_ref[...]
    vone = vone_ref[...]
    for h in range(MLA_HEADS):
        sl = slice(h * HEAD_GROUP, (h + 1) * HEAD_GROUP)
        qm_ref[:, sl] = _rope_group(q[:, sl], cosq, sinq).astype(bf)
        km_ref[:, sl] = (kn[:, sl] + kr).astype(bf)
        vm_ref[:, sl] = (vv[:, sl] + vone).astype(bf)


def _inproj(x2, nmix, win, qn, wuq, kvn, wuk, wuv, cosq, sinq, cosk, sink, vone, tm):
    t = x2.shape[0]
    n_tab = cosq.shape[0] // tm
    row = lambda i: (i, 0)
    tab = lambda i: (i % n_tab, 0)
    bf = jnp.bfloat16
    return pl.pallas_call(
        _inproj_kernel,
        grid=(t // tm,),
        in_specs=[
            pl.BlockSpec((tm, D_MODEL), row),
            _const_spec((1, D_MODEL)),
            _const_spec((D_MODEL, D_IN_EXT)),
            _const_spec((1, MLA_Q_RANK)),
            _const_spec((MLA_Q_RANK, MLA_QK)),
            _const_spec((1, MLA_KV_RANK)),
            _const_spec((MLA_KV_RANK, MLA_QK)),
            _const_spec((MLA_KV_RANK, MLA_QK)),
            pl.BlockSpec((tm, HEAD_GROUP), tab),
            pl.BlockSpec((tm, HEAD_GROUP), tab),
            pl.BlockSpec((tm, HEAD_GROUP), tab),
            pl.BlockSpec((tm, HEAD_GROUP), tab),
            _const_spec((1, HEAD_GROUP)),
        ],
        out_specs=[
            pl.BlockSpec((tm, 3 * NA_W), row),
            pl.BlockSpec((tm, MLA_QK), row),
            pl.BlockSpec((tm, MLA_QK), row),
            pl.BlockSpec((tm, MLA_QK), row),
            pl.BlockSpec((tm, 2 * D_MODEL), row),
        ],
        out_shape=[
            jax.ShapeDtypeStruct((t, 3 * NA_W), bf),
            jax.ShapeDtypeStruct((t, MLA_QK), bf),
            jax.ShapeDtypeStruct((t, MLA_QK), bf),
            jax.ShapeDtypeStruct((t, MLA_QK), bf),
            jax.ShapeDtypeStruct((t, 2 * D_MODEL), bf),
        ],
        compiler_params=pltpu.CompilerParams(
            dimension_semantics=("parallel",), vmem_limit_bytes=VMEM_LIMIT),
        name="inproj",
    )(x2, nmix, win, qn, wuq, kvn, wuk, wuv, cosq, sinq, cosk, sink, vone)


def _na_kernel(q_ref, k_ref, v_ref, km_ref, vm_ref, tbl_ref, o_ref, *, n_blocks):
    bf = jnp.bfloat16
    f32 = jnp.float32
    j = pl.program_id(2)
    rows = n_blocks * NA_ROWS_PER_BLOCK
    ws = jnp.clip(j * NA_ROWS_PER_BLOCK - NA_WIN_H // 2, 0, rows - NA_KEY_ROWS)
    start = pl.multiple_of(ws * GRID_W, GRID_W)
    variant = jnp.where(j == 0, 0, jnp.where(j == n_blocks - 1, 2, 1))
    kb = k_ref[pl.ds(start, NA_K), :]
    vb = v_ref[pl.ds(start, NA_K), :]
    kmeta = km_ref[...]
    vmeta = vm_ref[...]
    q2 = q_ref[...]
    lane = lax.broadcasted_iota(jnp.int32, q2.shape, 1)
    outs = []
    for hh in range(2):
        in_head = (lane >= hh * NA_HEAD_DIM) & (lane < (hh + 1) * NA_HEAD_DIM)
        qh = jnp.where(in_head, q2, jnp.zeros_like(q2))
        s_m = lax.dot_general(qh, kmeta, _NT, preferred_element_type=f32) + tbl_ref[variant, hh, :, :META_PAD]
        s_g = lax.dot_general(qh, kb, _NT, preferred_element_type=f32) + tbl_ref[variant, hh, :, META_PAD:]
        m = jnp.maximum(jnp.max(s_m, axis=1, keepdims=True), jnp.max(s_g, axis=1, keepdims=True))
        p_m = jnp.exp(s_m - m)
        p_g = jnp.exp(s_g - m)
        l = jnp.sum(p_m, axis=1, keepdims=True) + jnp.sum(p_g, axis=1, keepdims=True)
        o = (jnp.dot(p_m.astype(bf), vmeta, preferred_element_type=f32)
             + jnp.dot(p_g.astype(bf), vb, preferred_element_type=f32))
        outs.append(o / l)
    o_ref[...] = jnp.where(lane < NA_HEAD_DIM, outs[0], outs[1]).astype(bf)


def _na_attention(qkv, qkv_meta, tbl, batch, seq):
    n_blocks = seq // NA_Q
    n_pairs = NA_HEADS // 2
    kcol = NA_W // LANES
    vcol = 2 * NA_W // LANES
    return pl.pallas_call(
        functools.partial(_na_kernel, n_blocks=n_blocks),
        grid=(n_pairs, batch, n_blocks),
        in_specs=[
            pl.BlockSpec((NA_Q, LANES), lambda hp, b, j: (b * n_blocks + j, hp)),
            pl.BlockSpec((seq, LANES), lambda hp, b, j: (b, kcol + hp)),
            pl.BlockSpec((seq, LANES), lambda hp, b, j: (b, vcol + hp)),
            pl.BlockSpec((META_PAD, LANES), lambda hp, b, j: (0, kcol + hp)),
            pl.BlockSpec((META_PAD, LANES), lambda hp, b, j: (0, vcol + hp)),
            pl.BlockSpec((3, 2, NA_Q, META_PAD + NA_K), lambda hp, b, j: (0, hp, 0, 0)),
        ],
        out_specs=pl.BlockSpec((NA_Q, LANES), lambda hp, b, j: (b * n_blocks + j, hp)),
        out_shape=jax.ShapeDtypeStruct((batch * seq, NA_W), jnp.bfloat16),
        compiler_params=pltpu.CompilerParams(
            dimension_semantics=("parallel", "parallel", "arbitrary"), vmem_limit_bytes=VMEM_LIMIT),
        name="na_attn",
    )(qkv, qkv, qkv, qkv_meta, qkv_meta, tbl)


def _na_bias_table(rpb, seq):
    rows = seq // GRID_W
    n_blocks = rows // NA_ROWS_PER_BLOCK
    kh = min(NA_WIN_H, rows)
    rpb = rpb.astype(jnp.float32)
    side = GRID_W - NA_WIN_W
    rpb_pad = jnp.pad(rpb, ((0, 0), (0, 0), (side, side)))
    toep = jnp.stack([rpb_pad[:, :, GRID_W - 1 - qc:2 * GRID_W - 1 - qc] for qc in range(GRID_W)], axis=2)
    qc = np.arange(GRID_W)[:, None]
    kc = np.arange(GRID_W)[None, :]
    cstart = np.clip(qc - NA_WIN_W // 2, 0, GRID_W - NA_WIN_W)
    cvalid = (kc >= cstart) & (kc < cstart + NA_WIN_W)
    toep = jnp.where(jnp.asarray(cvalid), toep, NEG)
    variants = []
    for j in (0, 1, n_blocks - 1):
        r0 = j * NA_ROWS_PER_BLOCK
        ws = int(np.clip(r0 - NA_WIN_H // 2, 0, rows - NA_KEY_ROWS))
        per_row = []
        for rq in range(NA_ROWS_PER_BLOCK):
            r = r0 + rq
            rs = int(np.clip(r - kh // 2, 0, rows - kh))
            dr_lo = rs - r + NA_WIN_H - 1
            rk_lo = rs - ws
            blk = toep[:, dr_lo:dr_lo + kh].transpose(0, 2, 1, 3)
            blk = jnp.pad(blk, ((0, 0), (0, 0), (rk_lo, NA_KEY_ROWS - rk_lo - kh), (0, 0)),
                          constant_values=NEG)
            per_row.append(blk.reshape(NA_HEADS, GRID_W, NA_K))
        variants.append(jnp.concatenate(per_row, axis=1))
    grid_part = jnp.stack(variants)
    meta_part = np.where(np.arange(META_PAD) < N_META, 0.0, NEG).astype(np.float32)
    meta_part = jnp.broadcast_to(jnp.asarray(meta_part), (3, NA_HEADS, NA_Q, META_PAD))
    return jnp.concatenate([meta_part, grid_part], axis=-1)


def _mla_kernel(q_ref, k_ref, v_ref, km_ref, vm_ref, o_ref):
    bf = jnp.bfloat16
    f32 = jnp.float32
    tq = q_ref.shape[0]
    meta_lane = lax.broadcasted_iota(jnp.int32, (tq, META_PAD), 1)
    outs = []
    for hh in range(2):
        sl = slice(hh * HEAD_GROUP, (hh + 1) * HEAD_GROUP)
        qh = q_ref[:, sl]
        s0 = lax.dot_general(qh, km_ref[:, sl], _NT, preferred_element_type=f32)
        s0 = jnp.where(meta_lane < N_META, s0, NEG)
        s = lax.dot_general(qh, k_ref[:, sl], _NT, preferred_element_type=f32)
        m = jnp.maximum(jnp.max(s0, axis=1, keepdims=True), jnp.max(s, axis=1, keepdims=True))
        p0 = jnp.exp2(s0 - m).astype(bf)
        p = jnp.exp2(s - m).astype(bf)
        acc = (jnp.dot(p0, vm_ref[:, sl], preferred_element_type=f32)
               + jnp.dot(p, v_ref[:, sl], preferred_element_type=f32))
        outs.append(acc / acc[:, MLA_V_DIM:MLA_V_DIM + 1])
    lane = lax.broadcasted_iota(jnp.int32, (tq, HEAD_GROUP), 1)
    o_ref[...] = jnp.where(lane < MLA_V_DIM, outs[0], pltpu.roll(outs[1], MLA_V_DIM, 1)).astype(bf)


def _mla_attention(qm, km, vm, km_meta, vm_meta, batch, seq, tq):
    nq = seq // tq
    n_pairs = MLA_HEADS // 2
    pair = 2 * HEAD_GROUP
    return pl.pallas_call(
        _mla_kernel,
        grid=(batch, n_pairs, nq),
        in_specs=[
            pl.BlockSpec((tq, pair), lambda b, hp, i: (b * nq + i, hp)),
            pl.BlockSpec((seq, pair), lambda b, hp, i: (b, hp)),
            pl.BlockSpec((seq, pair), lambda b, hp, i: (b, hp)),
            pl.BlockSpec((META_PAD, pair), lambda b, hp, i: (0, hp)),
            pl.BlockSpec((META_PAD, pair), lambda b, hp, i: (0, hp)),
        ],
        out_specs=pl.BlockSpec((tq, 2 * MLA_V_DIM), lambda b, hp, i: (b * nq + i, hp)),
        out_shape=jax.ShapeDtypeStruct((batch * seq, MLA_HEADS * MLA_V_DIM), jnp.bfloat16),
        compiler_params=pltpu.CompilerParams(
            dimension_semantics=("parallel", "parallel", "arbitrary"), vmem_limit_bytes=VMEM_LIMIT),
        name="mla_attn",
    )(qm, km, vm, km_meta, vm_meta)


def _post_kernel(x_ref, ona_ref, omla_ref, g_ref, wna_ref, wmla_ref, wout_ref, nffn_ref,
                 wff1_ref, wff2_ref, nfin_ref, o_ref):
    bf = jnp.bfloat16
    f32 = jnp.float32
    a = jnp.dot(ona_ref[...], wna_ref[...], preferred_element_type=f32)
    b = jnp.dot(omla_ref[...], wmla_ref[...], preferred_element_type=f32)
    g_na = g_ref[:, :D_MODEL].astype(f32)
    g_mla = g_ref[:, D_MODEL:].astype(f32)
    merged = jax.nn.sigmoid(g_na) * a + jax.nn.sigmoid(g_mla) * b
    h = x_ref[...] + jnp.dot(merged.astype(bf), wout_ref[...], preferred_element_type=f32)
    fn = _rms(h, nffn_ref[...]).astype(bf)
    u = jnp.dot(fn, wff1_ref[...], preferred_element_type=f32)
    u = jnp.square(jnp.maximum(u, 0.0)).astype(bf)
    h = h + jnp.dot(u, wff2_ref[...], preferred_element_type=f32)
    o_ref[...] = _rms(h, nfin_ref[...])


def _post(x2, o_na, o_mla, g, wna, wmla, wout, nffn, wff1, wff2, nfin, tm):
    t = x2.shape[0]
    row = lambda i: (i, 0)
    return pl.pallas_call(
        _post_kernel,
        grid=(t // tm,),
        in_specs=[
            pl.BlockSpec((tm, D_MODEL), row),
            pl.BlockSpec((tm, NA_W), row),
            pl.BlockSpec((tm, MLA_HEADS * MLA_V_DIM), row),
            pl.BlockSpec((tm, 2 * D_MODEL), row),
            _const_spec((NA_W, D_MODEL)),
            _const_spec((MLA_HEADS * MLA_V_DIM, D_MODEL)),
            _const_spec((D_MODEL, D_MODEL)),
            _const_spec((1, D_MODEL)),
            _const_spec((D_MODEL, D_FF)),
            _const_spec((D_FF, D_MODEL)),
            _const_spec((1, D_MODEL)),
        ],
        out_specs=pl.BlockSpec((tm, D_MODEL), row),
        out_shape=jax.ShapeDtypeStruct((t, D_MODEL), jnp.float32),
        compiler_params=pltpu.CompilerParams(
            dimension_semantics=("parallel",), vmem_limit_bytes=VMEM_LIMIT),
        name="post",
    )(x2, o_na, o_mla, g, wna, wmla, wout, nffn, wff1, wff2, nfin)


def _head_groups(w, per_head, lane0):
    k = w.shape[0]
    w = w.reshape(k, MLA_HEADS, per_head)
    w = jnp.pad(w, ((0, 0), (0, 0), (lane0, HEAD_GROUP - lane0 - per_head)))
    return w.reshape(k, MLA_HEADS * HEAD_GROUP)


def _rope_tables(pos, scale):
    half = MLA_ROPE_DIM // 2
    inv_freq = 1.0 / (ROPE_THETA ** (jnp.arange(0, MLA_ROPE_DIM, 2, dtype=jnp.float32) / MLA_ROPE_DIM))
    ang = pos[:, None] * inv_freq[None, :]
    cos, sin = jnp.cos(ang), jnp.sin(ang)
    n = pos.shape[0]
    ones = jnp.ones((n, ROPE_LANE0), jnp.float32)
    tail = HEAD_GROUP - ROPE_LANE0 - MLA_ROPE_DIM
    cos_t = jnp.concatenate([ones, cos, cos, jnp.ones((n, tail), jnp.float32)], axis=1) * scale
    sin_t = jnp.concatenate([0 * ones, -sin, sin, jnp.zeros((n, tail), jnp.float32)], axis=1) * scale
    return cos_t, sin_t


def kernel(x, meta, norm_mix, w_in, na_rpb, mla_q_norm, w_uq, mla_kv_norm, w_ukv, w_na_out, w_mla_out,
           w_out, norm_ffn, w_ff1, w_ff2, norm_final):
    assert norm_mix.shape[0] == 1, "single-layer block"
    batch, seq, d = x.shape
    bf = jnp.bfloat16
    f32 = jnp.float32
    x2 = x.reshape(batch * seq, d)

    wi = w_in[0]
    kr_cols = jnp.pad(wi[:, C_KR:C_KR + MLA_ROPE_DIM],
                      ((0, 0), (ROPE_LANE0, HEAD_GROUP - ROPE_LANE0 - MLA_ROPE_DIM)))
    win = jnp.concatenate([wi[:, :C_KR], kr_cols, wi[:, C_KR + MLA_ROPE_DIM:]], axis=1).astype(bf)
    wuq = _head_groups(w_uq[0], MLA_NOPE_DIM + MLA_ROPE_DIM, 0).astype(bf)
    wukv = w_ukv[0].reshape(MLA_KV_RANK, MLA_HEADS, MLA_NOPE_DIM + MLA_V_DIM)
    wuk = _head_groups(wukv[:, :, :MLA_NOPE_DIM].reshape(MLA_KV_RANK, -1), MLA_NOPE_DIM, 0).astype(bf)
    wuv = _head_groups(wukv[:, :, MLA_NOPE_DIM:].reshape(MLA_KV_RANK, -1), MLA_V_DIM, 0).astype(bf)
    vone = (jnp.arange(HEAD_GROUP) == MLA_V_DIM).astype(f32)[None]

    mla_scale = (MLA_NOPE_DIM + MLA_ROPE_DIM) ** -0.5 * float(np.log2(np.e))
    pos_tok = jnp.arange(N_META, N_META + seq, dtype=f32)
    pos_meta = jnp.arange(N_META, dtype=f32)
    cosq, sinq = _rope_tables(pos_tok, mla_scale)
    cosk, sink = _rope_tables(pos_tok, 1.0)
    cosq_m, sinq_m = _rope_tables(pos_meta, mla_scale)
    cosk_m, sink_m = _rope_tables(pos_meta, 1.0)

    small = (norm_mix[0][None], win, mla_q_norm[0][None], wuq, mla_kv_norm[0][None], wuk, wuv)
    qkv, qm, km, vm, g = _inproj(x2, *small, cosq, sinq, cosk, sink, vone, tm=512)
    qkv_meta, _, km_meta, vm_meta, _ = _inproj(meta.astype(f32), *small, cosq_m, sinq_m, cosk_m, sink_m, vone,
                                               tm=N_META)
    pad_rows = ((0, META_PAD - N_META), (0, 0))
    qkv_meta = jnp.pad(qkv_meta, pad_rows)
    km_meta = jnp.pad(km_meta, pad_rows)
    vm_meta = jnp.pad(vm_meta, pad_rows)

    tbl = _na_bias_table(na_rpb[0], seq)
    o_na = _na_attention(qkv, qkv_meta, tbl, batch, seq)
    o_mla = _mla_attention(qm, km, vm, km_meta, vm_meta, batch, seq, tq=256)

    out = _post(x2, o_na, o_mla, g, w_na_out[0].astype(bf), w_mla_out[0].astype(bf), w_out[0].astype(bf),
                norm_ffn[0][None], w_ff1[0].astype(bf), w_ff2[0].astype(bf), norm_final[None], tm=512)
    return out.reshape(batch, seq, d)
```

```python
import functools

import numpy as np
import jax
import jax.numpy as jnp
from jax import lax
from jax.experimental import pallas as pl
from jax.experimental.pallas import tpu as pltpu

D_MODEL = 1024
N_META = 16
GRID_W = 64
NA_HEADS = 8
NA_HEAD_DIM = 64
NA_WIN_H = 8
NA_WIN_W = 16
NA_W = NA_HEADS * NA_HEAD_DIM
MLA_HEADS = 8
MLA_NOPE_DIM = 64
MLA_ROPE_DIM = 32
MLA_V_DIM = 64
MLA_Q_RANK = 384
MLA_KV_RANK = 256
ROPE_THETA = 10000.0
D_FF = 4 * D_MODEL
EPS = 1e-6

LANES = 128
HEAD_GROUP = 128
ROPE_LANE0 = MLA_NOPE_DIM
MLA_QK = MLA_HEADS * HEAD_GROUP
NEG = -1e30

C_QKV = 0
C_CQ = 3 * NA_W
C_CKV = C_CQ + MLA_Q_RANK
C_KR = C_CKV + MLA_KV_RANK
C_G = C_KR + HEAD_GROUP
D_IN_EXT = C_G + 2 * D_MODEL

NA_ROWS_PER_BLOCK = 4
NA_KEY_ROWS = NA_ROWS_PER_BLOCK + NA_WIN_H
NA_Q = NA_ROWS_PER_BLOCK * GRID_W
NA_K = NA_KEY_ROWS * GRID_W
NA_BLOCKS_PER_STEP = 4
META_PAD = LANES
LOG2E = float(np.log2(np.e))
MLA_TK = 512
MLA_SUB = 256

VMEM_LIMIT = 56 * 1024 * 1024

_NT = (((1,), (1,)), ((), ()))


def _const_spec(shape):
    nd = len(shape)
    return pl.BlockSpec(shape, lambda *_: (0,) * nd, pipeline_mode=pl.Buffered(1))


def _rms(x, g):
    return x * lax.rsqrt(jnp.mean(x * x, axis=-1, keepdims=True) + EPS) * g


def _rope_group(x, cos, sin):
    lane = lax.broadcasted_iota(jnp.int32, x.shape, 1)
    partner = jnp.where(lane < ROPE_LANE0 + MLA_ROPE_DIM // 2,
                        pltpu.roll(x, HEAD_GROUP - MLA_ROPE_DIM // 2, 1),
                        pltpu.roll(x, MLA_ROPE_DIM // 2, 1))
    return x * cos + partner * sin


def _inproj_kernel(x_ref, nmix_ref, win_ref, qn_ref, wuq_ref, kvn_ref, wuk_ref, wuv_ref,
                   cosq_ref, sinq_ref, cosk_ref, sink_ref, vone_ref,
                   qkv_ref, qm_ref, km_ref, vm_ref, g_ref):
    bf = jnp.bfloat16
    f32 = jnp.float32
    hn = _rms(x_ref[...], nmix_ref[...]).astype(bf)

    qkv = jnp.dot(hn, win_ref[:, C_QKV:C_CQ], preferred_element_type=f32)
    qkv_ref[:, :NA_W] = (qkv[:, :NA_W] * (NA_HEAD_DIM ** -0.5 * LOG2E)).astype(bf)
    qkv_ref[:, NA_W:] = qkv[:, NA_W:].astype(bf)

    g_ref[...] = jnp.dot(hn, win_ref[:, C_G:], preferred_element_type=f32).astype(bf)

    lat = jnp.dot(hn, win_ref[:, C_CQ:C_G], preferred_element_type=f32)
    cq = _rms(lat[:, :MLA_Q_RANK], qn_ref[...]).astype(bf)
    ckv = _rms(lat[:, MLA_Q_RANK:MLA_Q_RANK + MLA_KV_RANK], kvn_ref[...]).astype(bf)
    kr = lat[:, MLA_Q_RANK + MLA_KV_RANK:]
    kr = _rope_group(kr, cosk_ref[...], sink_ref[...])

    q = jnp.dot(cq, wuq_ref[...], preferred_element_type=f32)
    kn = jnp.dot(ckv, wuk_ref[...], preferred_element_type=f32)
    vv = jnp.dot(ckv, wuv_ref[...], preferred_element_type=f32)
    cosq = cosq_ref[...]
    sinq = sinq_ref[...]
    vone = vone_ref[...]
    for h in range(MLA_HEADS):
        sl = slice(h * HEAD_GROUP, (h + 1) * HEAD_GROUP)
        qm_ref[:, sl] = _rope_group(q[:, sl], cosq, sinq).astype(bf)
        km_ref[:, sl] = (kn[:, sl] + kr).astype(bf)
        vm_ref[:, sl] = (vv[:, sl] + vone).astype(bf)


def _inproj(x2, nmix, win, qn, wuq, kvn, wuk, wuv, cosq, sinq, cosk, sink, vone, tm):
    t = x2.shape[0]
    n_tab = cosq.shape[0] // tm
    row = lambda i: (i, 0)
    tab = lambda i: (i % n_tab, 0)
    bf = jnp.bfloat16
    return pl.pallas_call(
        _inproj_kernel,
        grid=(t // tm,),
        in_specs=[
            pl.BlockSpec((tm, D_MODEL), row),
            _const_spec((1, D_MODEL)),
            _const_spec((D_MODEL, D_IN_EXT)),
            _const_spec((1, MLA_Q_RANK)),
            _const_spec((MLA_Q_RANK, MLA_QK)),
            _const_spec((1, MLA_KV_RANK)),
            _const_spec((MLA_KV_RANK, MLA_QK)),
            _const_spec((MLA_KV_RANK, MLA_QK)),
            pl.BlockSpec((tm, HEAD_GROUP), tab),
            pl.BlockSpec((tm, HEAD_GROUP), tab),
            pl.BlockSpec((tm, HEAD_GROUP), tab),
            pl.BlockSpec((tm, HEAD_GROUP), tab),
            _const_spec((1, HEAD_GROUP)),
        ],
        out_specs=[
            pl.BlockSpec((tm, 3 * NA_W), row),
            pl.BlockSpec((tm, MLA_QK), row),
            pl.BlockSpec((tm, MLA_QK), row),
            pl.BlockSpec((tm, MLA_QK), row),
            pl.BlockSpec((tm, 2 * D_MODEL), row),
        ],
        out_shape=[
            jax.ShapeDtypeStruct((t, 3 * NA_W), bf),
            jax.ShapeDtypeStruct((t, MLA_QK), bf),
            jax.ShapeDtypeStruct((t, MLA_QK), bf),
            jax.ShapeDtypeStruct((t, MLA_QK), bf),
            jax.ShapeDtypeStruct((t, 2 * D_MODEL), bf),
        ],
        compiler_params=pltpu.CompilerParams(
            dimension_semantics=("parallel",), vmem_limit_bytes=VMEM_LIMIT),
        name="inproj",
    )(x2, nmix, win, qn, wuq, kvn, wuk, wuv, cosq, sinq, cosk, sink, vone)


def _fold_lanes(x, op):
    out = x[:, :LANES]
    for t in range(1, x.shape[1] // LANES):
        out = op(out, x[:, t * LANES:(t + 1) * LANES])
    return out


def _na_kernel(q_ref, k_ref, v_ref, km_ref, vm_ref, tbl_ref, o_ref, s_ref, p_ref, *, n_blocks):
    bf = jnp.bfloat16
    f32 = jnp.float32
    nb = q_ref.shape[0] // NA_Q
    rows = n_blocks * NA_ROWS_PER_BLOCK
    units = [(i, hh) for i in range(nb) for hh in range(2)]
    n_units = len(units)
    lane = lax.broadcasted_iota(jnp.int32, (NA_Q, LANES), 1)
    in_head = (lane < NA_HEAD_DIM, lane >= NA_HEAD_DIM)

    def window(i):
        j = pl.program_id(2) * nb + i
        ws = jnp.clip(j * NA_ROWS_PER_BLOCK - NA_WIN_H // 2, 0, rows - NA_KEY_ROWS)
        variant = jnp.where(j == 0, 0, jnp.where(j == n_blocks - 1, 2, 1))
        return pl.multiple_of(ws * GRID_W, GRID_W), variant

    wins = [window(i) for i in range(nb)]

    def scores(u):
        i, hh = units[u]
        start, variant = wins[i]
        q2 = q_ref[i * NA_Q:(i + 1) * NA_Q, :]
        qh = jnp.where(in_head[hh], q2, jnp.zeros_like(q2))
        s_m = lax.dot_general(qh, km_ref[...], _NT, preferred_element_type=f32)
        s_m = s_m + tbl_ref[variant, hh, :, :META_PAD]
        s_g = lax.dot_general(qh, k_ref[pl.ds(start, NA_K), :], _NT, preferred_element_type=f32)
        s_g = s_g + tbl_ref[variant, hh, :, META_PAD:]
        s_ref[u % 2, :, :META_PAD] = s_m
        s_ref[u % 2, :, META_PAD:] = s_g
        return jnp.max(jnp.maximum(s_m, _fold_lanes(s_g, jnp.maximum)), axis=1, keepdims=True)

    def probs(u, m):
        p = jnp.exp2(s_ref[u % 2] - m)
        p_ref[u % 2] = p.astype(bf)
        return jnp.sum(_fold_lanes(p, jnp.add), axis=1, keepdims=True)

    def weighted(u):
        start = wins[units[u][0]][0]
        return (jnp.dot(p_ref[u % 2, :, :META_PAD], vm_ref[...], preferred_element_type=f32)
                + jnp.dot(p_ref[u % 2, :, META_PAD:], v_ref[pl.ds(start, NA_K), :], preferred_element_type=f32))

    row_max, denom, out = {}, {}, {}
    for k in range(n_units + 2):
        if k < n_units:
            row_max[k] = scores(k)
        if 1 <= k <= n_units:
            denom[k - 1] = probs(k - 1, row_max[k - 1])
        if k >= 2:
            out[k - 2] = weighted(k - 2) / denom[k - 2]
    for i in range(nb):
        o_ref[i * NA_Q:(i + 1) * NA_Q, :] = jnp.where(in_head[0], out[2 * i], out[2 * i + 1]).astype(bf)


def _na_attention(qkv, qkv_meta, tbl, batch, seq):
    n_blocks = seq // NA_Q
    n_steps = n_blocks // NA_BLOCKS_PER_STEP
    n_pairs = NA_HEADS // 2
    kcol = NA_W // LANES
    vcol = 2 * NA_W // LANES
    tq = NA_BLOCKS_PER_STEP * NA_Q
    nk = META_PAD + NA_K
    return pl.pallas_call(
        functools.partial(_na_kernel, n_blocks=n_blocks),
        grid=(n_pairs, batch, n_steps),
        in_specs=[
            pl.BlockSpec((tq, LANES), lambda hp, b, j: (b * n_steps + j, hp)),
            pl.BlockSpec((seq, LANES), lambda hp, b, j: (b, kcol + hp)),
            pl.BlockSpec((seq, LANES), lambda hp, b, j: (b, vcol + hp)),
            pl.BlockSpec((META_PAD, LANES), lambda hp, b, j: (0, kcol + hp)),
            pl.BlockSpec((META_PAD, LANES), lambda hp, b, j: (0, vcol + hp)),
            pl.BlockSpec((3, 2, NA_Q, nk), lambda hp, b, j: (0, hp, 0, 0)),
        ],
        out_specs=pl.BlockSpec((tq, LANES), lambda hp, b, j: (b * n_steps + j, hp)),
        out_shape=jax.ShapeDtypeStruct((batch * seq, NA_W), jnp.bfloat16),
        scratch_shapes=[pltpu.VMEM((2, NA_Q, nk), jnp.float32), pltpu.VMEM((2, NA_Q, nk), jnp.bfloat16)],
        compiler_params=pltpu.CompilerParams(
            dimension_semantics=("parallel", "parallel", "arbitrary"), vmem_limit_bytes=VMEM_LIMIT),
        name="na_attn",
    )(qkv, qkv, qkv, qkv_meta, qkv_meta, tbl)


def _na_bias_table(rpb, seq):
    rows = seq // GRID_W
    n_blocks = rows // NA_ROWS_PER_BLOCK
    kh = min(NA_WIN_H, rows)
    rpb = rpb.astype(jnp.float32) * LOG2E
    side = GRID_W - NA_WIN_W
    rpb_pad = jnp.pad(rpb, ((0, 0), (0, 0), (side, side)))
    toep = jnp.stack([rpb_pad[:, :, GRID_W - 1 - qc:2 * GRID_W - 1 - qc] for qc in range(GRID_W)], axis=2)
    qc = np.arange(GRID_W)[:, None]
    kc = np.arange(GRID_W)[None, :]
    cstart = np.clip(qc - NA_WIN_W // 2, 0, GRID_W - NA_WIN_W)
    cvalid = (kc >= cstart) & (kc < cstart + NA_WIN_W)
    toep = jnp.where(jnp.asarray(cvalid), toep, NEG)
    variants = []
    for j in (0, 1, n_blocks - 1):
        r0 = j * NA_ROWS_PER_BLOCK
        ws = int(np.clip(r0 - NA_WIN_H // 2, 0, rows - NA_KEY_ROWS))
        per_row = []
        for rq in range(NA_ROWS_PER_BLOCK):
            r = r0 + rq
            rs = int(np.clip(r - kh // 2, 0, rows - kh))
            dr_lo = rs - r + NA_WIN_H - 1
            rk_lo = rs - ws
            blk = toep[:, dr_lo:dr_lo + kh].transpose(0, 2, 1, 3)
            blk = jnp.pad(blk, ((0, 0), (0, 0), (rk_lo, NA_KEY_ROWS - rk_lo - kh), (0, 0)),
                          constant_values=NEG)
            per_row.append(blk.reshape(NA_HEADS, GRID_W, NA_K))
        variants.append(jnp.concatenate(per_row, axis=1))
    grid_part = jnp.stack(variants)
    meta_part = np.where(np.arange(META_PAD) < N_META, 0.0, NEG).astype(np.float32)
    meta_part = jnp.broadcast_to(jnp.asarray(meta_part), (3, NA_HEADS, NA_Q, META_PAD))
    return jnp.concatenate([meta_part, grid_part], axis=-1)


def _mla_kernel(q_ref, k_ref, v_ref, km_ref, vm_ref, o_ref, s_ref, p_ref):
    bf = jnp.bfloat16
    f32 = jnp.float32
    n_sub = q_ref.shape[0] // MLA_SUB
    n_chunks = k_ref.shape[0] // MLA_TK
    meta_lane = lax.broadcasted_iota(jnp.int32, (MLA_SUB, META_PAD), 1)
    head = (slice(0, HEAD_GROUP), slice(HEAD_GROUP, 2 * HEAD_GROUP))
    chunk = [slice(c * MLA_TK, (c + 1) * MLA_TK) for c in range(n_chunks)]
    units = [(slice(i * MLA_SUB, (i + 1) * MLA_SUB), hh) for i in range(n_sub) for hh in range(2)]
    n_units = len(units)

    def meta_scores(u):
        rows, hh = units[u]
        s0 = lax.dot_general(q_ref[rows, head[hh]], km_ref[:, head[hh]], _NT, preferred_element_type=f32)
        return jnp.where(meta_lane < N_META, s0, NEG)

    def scores(u, c):
        rows, hh = units[u]
        s = lax.dot_general(q_ref[rows, head[hh]], k_ref[chunk[c], head[hh]], _NT, preferred_element_type=f32)
        s_ref[u % 2, :, chunk[c]] = s
        mx = s[:, :LANES]
        for t in range(1, MLA_TK // LANES):
            mx = jnp.maximum(mx, s[:, t * LANES:(t + 1) * LANES])
        return mx

    def probs(u, c, m):
        p_ref[u % 2, :, chunk[c]] = jnp.exp2(s_ref[u % 2, :, chunk[c]] - m).astype(bf)

    def weighted(u, c):
        hh = units[u][1]
        return jnp.dot(p_ref[u % 2, :, chunk[c]], v_ref[chunk[c], head[hh]], preferred_element_type=f32)

    s_meta, row_max, acc = {}, {}, {}
    for k in range(n_units + 2):
        if k < n_units:
            s_meta[k] = meta_scores(k)
            mx = s_meta[k]
        if k >= 2:
            p_meta = jnp.exp2(s_meta[k - 2] - row_max[k - 2]).astype(bf)
            acc[k - 2] = jnp.dot(p_meta, vm_ref[:, head[units[k - 2][1]]], preferred_element_type=f32)
        for c in range(n_chunks):
            if k < n_units:
                mx = jnp.maximum(mx, scores(k, c))
            if 1 <= k <= n_units:
                probs(k - 1, c, row_max[k - 1])
            if k >= 2:
                acc[k - 2] = acc[k - 2] + weighted(k - 2, c)
        if k < n_units:
            row_max[k] = jnp.max(mx, axis=1, keepdims=True)

    lane = lax.broadcasted_iota(jnp.int32, (MLA_SUB, HEAD_GROUP), 1)
    for i in range(n_sub):
        o0, o1 = [acc[2 * i + hh] / acc[2 * i + hh][:, MLA_V_DIM:MLA_V_DIM + 1] for hh in range(2)]
        o_ref[units[2 * i][0], :] = jnp.where(lane < MLA_V_DIM, o0, pltpu.roll(o1, MLA_V_DIM, 1)).astype(bf)


def _mla_attention(qm, km, vm, km_meta, vm_meta, batch, seq, tq):
    nq = seq // tq
    n_pairs = MLA_HEADS // 2
    pair = 2 * HEAD_GROUP
    return pl.pallas_call(
        _mla_kernel,
        grid=(batch, n_pairs, nq),
        in_specs=[
            pl.BlockSpec((tq, pair), lambda b, hp, i: (b * nq + i, hp)),
            pl.BlockSpec((seq, pair), lambda b, hp, i: (b, hp)),
            pl.BlockSpec((seq, pair), lambda b, hp, i: (b, hp)),
            pl.BlockSpec((META_PAD, pair), lambda b, hp, i: (0, hp)),
            pl.BlockSpec((META_PAD, pair), lambda b, hp, i: (0, hp)),
        ],
        out_specs=pl.BlockSpec((tq, 2 * MLA_V_DIM), lambda b, hp, i: (b * nq + i, hp)),
        out_shape=jax.ShapeDtypeStruct((batch * seq, MLA_HEADS * MLA_V_DIM), jnp.bfloat16),
        scratch_shapes=[pltpu.VMEM((2, MLA_SUB, seq), jnp.float32), pltpu.VMEM((2, MLA_SUB, seq), jnp.bfloat16)],
        compiler_params=pltpu.CompilerParams(
            dimension_semantics=("parallel", "parallel", "arbitrary"), vmem_limit_bytes=VMEM_LIMIT),
        name="mla_attn",
    )(qm, km, vm, km_meta, vm_meta)


def _post_kernel(x_ref, ona_ref, omla_ref, g_ref, wna_ref, wmla_ref, wout_ref, nffn_ref,
                 wff1_ref, wff2_ref, nfin_ref, o_ref):
    bf = jnp.bfloat16
    f32 = jnp.float32
    a = jnp.dot(ona_ref[...], wna_ref[...], preferred_element_type=f32)
    b = jnp.dot(omla_ref[...], wmla_ref[...], preferred_element_type=f32)
    g_na = g_ref[:, :D_MODEL].astype(f32)
    g_mla = g_ref[:, D_MODEL:].astype(f32)
    merged = jax.nn.sigmoid(g_na) * a + jax.nn.sigmoid(g_mla) * b
    h = x_ref[...] + jnp.dot(merged.astype(bf), wout_ref[...], preferred_element_type=f32)
    fn = _rms(h, nffn_ref[...]).astype(bf)
    u = jnp.dot(fn, wff1_ref[...], preferred_element_type=f32)
    u = jnp.square(jnp.maximum(u, 0.0)).astype(bf)
    h = h + jnp.dot(u, wff2_ref[...], preferred_element_type=f32)
    o_ref[...] = _rms(h, nfin_ref[...])


def _post(x2, o_na, o_mla, g, wna, wmla, wout, nffn, wff1, wff2, nfin, tm):
    t = x2.shape[0]
    row = lambda i: (i, 0)
    return pl.pallas_call(
        _post_kernel,
        grid=(t // tm,),
        in_specs=[
            pl.BlockSpec((tm, D_MODEL), row),
            pl.BlockSpec((tm, NA_W), row),
            pl.BlockSpec((tm, MLA_HEADS * MLA_V_DIM), row),
            pl.BlockSpec((tm, 2 * D_MODEL), row),
            _const_spec((NA_W, D_MODEL)),
            _const_spec((MLA_HEADS * MLA_V_DIM, D_MODEL)),
            _const_spec((D_MODEL, D_MODEL)),
            _const_spec((1, D_MODEL)),
            _const_spec((D_MODEL, D_FF)),
            _const_spec((D_FF, D_MODEL)),
            _const_spec((1, D_MODEL)),
        ],
        out_specs=pl.BlockSpec((tm, D_MODEL), row),
        out_shape=jax.ShapeDtypeStruct((t, D_MODEL), jnp.float32),
        compiler_params=pltpu.CompilerParams(
            dimension_semantics=("parallel",), vmem_limit_bytes=VMEM_LIMIT),
        name="post",
    )(x2, o_na, o_mla, g, wna, wmla, wout, nffn, wff1, wff2, nfin)


def _head_groups(w, per_head, lane0):
    k = w.shape[0]
    w = w.reshape(k, MLA_HEADS, per_head)
    w = jnp.pad(w, ((0, 0), (0, 0), (lane0, HEAD_GROUP - lane0 - per_head)))
    return w.reshape(k, MLA_HEADS * HEAD_GROUP)


def _rope_tables(pos, scale):
    half = MLA_ROPE_DIM // 2
    inv_freq = 1.0 / (ROPE_THETA ** (jnp.arange(0, MLA_ROPE_DIM, 2, dtype=jnp.float32) / MLA_ROPE_DIM))
    ang = pos[:, None] * inv_freq[None, :]
    cos, sin = jnp.cos(ang), jnp.sin(ang)
    n = pos.shape[0]
    ones = jnp.ones((n, ROPE_LANE0), jnp.float32)
    tail = HEAD_GROUP - ROPE_LANE0 - MLA_ROPE_DIM
    cos_t = jnp.concatenate([ones, cos, cos, jnp.ones((n, tail), jnp.float32)], axis=1) * scale
    sin_t = jnp.concatenate([0 * ones, -sin, sin, jnp.zeros((n, tail), jnp.float32)], axis=1) * scale
    return cos_t, sin_t


def kernel(x, meta, norm_mix, w_in, na_rpb, mla_q_norm, w_uq, mla_kv_norm, w_ukv, w_na_out, w_mla_out,
           w_out, norm_ffn, w_ff1, w_ff2, norm_final):
    assert norm_mix.shape[0] == 1, "single-layer block"
    batch, seq, d = x.shape
    bf = jnp.bfloat16
    f32 = jnp.float32
    x2 = x.reshape(batch * seq, d)

    wi = w_in[0]
    kr_cols = jnp.pad(wi[:, C_KR:C_KR + MLA_ROPE_DIM],
                      ((0, 0), (ROPE_LANE0, HEAD_GROUP - ROPE_LANE0 - MLA_ROPE_DIM)))
    win = jnp.concatenate([wi[:, :C_KR], kr_cols, wi[:, C_KR + MLA_ROPE_DIM:]], axis=1).astype(bf)
    wuq = _head_groups(w_uq[0], MLA_NOPE_DIM + MLA_ROPE_DIM, 0).astype(bf)
    wukv = w_ukv[0].reshape(MLA_KV_RANK, MLA_HEADS, MLA_NOPE_DIM + MLA_V_DIM)
    wuk = _head_groups(wukv[:, :, :MLA_NOPE_DIM].reshape(MLA_KV_RANK, -1), MLA_NOPE_DIM, 0).astype(bf)
    wuv = _head_groups(wukv[:, :, MLA_NOPE_DIM:].reshape(MLA_KV_RANK, -1), MLA_V_DIM, 0).astype(bf)
    vone = (jnp.arange(HEAD_GROUP) == MLA_V_DIM).astype(f32)[None]

    mla_scale = (MLA_NOPE_DIM + MLA_ROPE_DIM) ** -0.5 * LOG2E
    pos_tok = jnp.arange(N_META, N_META + seq, dtype=f32)
    pos_meta = jnp.arange(N_META, dtype=f32)
    cosq, sinq = _rope_tables(pos_tok, mla_scale)
    cosk, sink = _rope_tables(pos_tok, 1.0)
    cosq_m, sinq_m = _rope_tables(pos_meta, mla_scale)
    cosk_m, sink_m = _rope_tables(pos_meta, 1.0)

    small = (norm_mix[0][None], win, mla_q_norm[0][None], wuq, mla_kv_norm[0][None], wuk, wuv)
    qkv, qm, km, vm, g = _inproj(x2, *small, cosq, sinq, cosk, sink, vone, tm=512)
    qkv_meta, _, km_meta, vm_meta, _ = _inproj(meta.astype(f32), *small, cosq_m, sinq_m, cosk_m, sink_m, vone,
                                               tm=N_META)
    pad_rows = ((0, META_PAD - N_META), (0, 0))
    qkv_meta = jnp.pad(qkv_meta, pad_rows)
    km_meta = jnp.pad(km_meta, pad_rows)
    vm_meta = jnp.pad(vm_meta, pad_rows)

    tbl = _na_bias_table(na_rpb[0], seq)
    o_na = _na_attention(qkv, qkv_meta, tbl, batch, seq)
    o_mla = _mla_attention(qm, km, vm, km_meta, vm_meta, batch, seq, tq=1024)

    out = _post(x2, o_na, o_mla, g, w_na_out[0].astype(bf), w_mla_out[0].astype(bf), w_out[0].astype(bf),
                norm_ffn[0][None], w_ff1[0].astype(bf), w_ff2[0].astype(bf), norm_final[None], tm=512)
    return out.reshape(batch, seq, d)
```

```python
import functools

import numpy as np
import jax
import jax.numpy as jnp
from jax import lax
from jax.experimental import pallas as pl
from jax.experimental.pallas import tpu as pltpu

D_MODEL = 1024
N_META = 16
GRID_W = 64
NA_HEADS = 8
NA_HEAD_DIM = 64
NA_WIN_H = 8
NA_WIN_W = 16
NA_W = NA_HEADS * NA_HEAD_DIM
MLA_HEADS = 8
MLA_NOPE_DIM = 64
MLA_ROPE_DIM = 32
MLA_V_DIM = 64
MLA_Q_RANK = 384
MLA_KV_RANK = 256
ROPE_THETA = 10000.0
D_FF = 4 * D_MODEL
EPS = 1e-6

LANES = 128
SUBLANES = 8
HEAD_GROUP = 128
ROPE_LANE0 = MLA_NOPE_DIM
MLA_QK = MLA_HEADS * HEAD_GROUP
NEG = -1e30

C_QKV = 0
C_CQ = 3 * NA_W
C_CKV = C_CQ + MLA_Q_RANK
C_KR = C_CKV + MLA_KV_RANK
C_G = C_KR + HEAD_GROUP
D_IN_EXT = C_G + 2 * D_MODEL

NA_ROWS_PER_BLOCK = 4
NA_KEY_ROWS = NA_ROWS_PER_BLOCK + NA_WIN_H
NA_Q = NA_ROWS_PER_BLOCK * GRID_W
NA_K = NA_KEY_ROWS * GRID_W
NA_BLOCKS_PER_STEP = 4
META_PAD = LANES
LOG2E = float(np.log2(np.e))
MLA_TK = 512
MLA_SUB = 512

VMEM_LIMIT = 56 * 1024 * 1024

_NT = (((1,), (1,)), ((), ()))


def _const_spec(shape):
    nd = len(shape)
    return pl.BlockSpec(shape, lambda *_: (0,) * nd, pipeline_mode=pl.Buffered(1))


def _rms(x, g):
    return x * lax.rsqrt(jnp.mean(x * x, axis=-1, keepdims=True) + EPS) * g


def _rope_group(x, cos, sin):
    lane = lax.broadcasted_iota(jnp.int32, x.shape, 1)
    partner = jnp.where(lane < ROPE_LANE0 + MLA_ROPE_DIM // 2,
                        pltpu.roll(x, HEAD_GROUP - MLA_ROPE_DIM // 2, 1),
                        pltpu.roll(x, MLA_ROPE_DIM // 2, 1))
    return x * cos + partner * sin


def _inproj_kernel(x_ref, nmix_ref, win_ref, qn_ref, wuq_ref, kvn_ref, wuk_ref, wuv_ref,
                   cosq_ref, sinq_ref, cosk_ref, sink_ref, vone_ref,
                   qkv_ref, qm_ref, km_ref, vt_ref, g_ref):
    bf = jnp.bfloat16
    f32 = jnp.float32
    hn = _rms(x_ref[...], nmix_ref[...]).astype(bf)

    qkv = jnp.dot(hn, win_ref[:, C_QKV:C_CQ], preferred_element_type=f32)
    qkv_ref[:, :NA_W] = (qkv[:, :NA_W] * (NA_HEAD_DIM ** -0.5 * LOG2E)).astype(bf)
    qkv_ref[:, NA_W:] = qkv[:, NA_W:].astype(bf)

    g_ref[...] = jnp.dot(hn, win_ref[:, C_G:], preferred_element_type=f32).astype(bf)

    lat = jnp.dot(hn, win_ref[:, C_CQ:C_G], preferred_element_type=f32)
    cq = _rms(lat[:, :MLA_Q_RANK], qn_ref[...]).astype(bf)
    ckv = _rms(lat[:, MLA_Q_RANK:MLA_Q_RANK + MLA_KV_RANK], kvn_ref[...]).astype(bf)
    kr = lat[:, MLA_Q_RANK + MLA_KV_RANK:]
    kr = _rope_group(kr, cosk_ref[...], sink_ref[...])

    q = jnp.dot(cq, wuq_ref[...], preferred_element_type=f32)
    kn = jnp.dot(ckv, wuk_ref[...], preferred_element_type=f32)
    vv = jnp.dot(ckv, wuv_ref[...], preferred_element_type=f32)
    cosq = cosq_ref[...]
    sinq = sinq_ref[...]
    for h in range(MLA_HEADS):
        sl = slice(h * HEAD_GROUP, (h + 1) * HEAD_GROUP)
        qm_ref[:, sl] = _rope_group(q[:, sl], cosq, sinq).astype(bf)
        km_ref[:, sl] = (kn[:, sl] + kr).astype(bf)
    vt_ref[...] = (vv + vone_ref[...]).T.astype(bf)


def _inproj(x2, nmix, win, qn, wuq, kvn, wuk, wuv, cosq, sinq, cosk, sink, vone, tm):
    t = x2.shape[0]
    n_tab = cosq.shape[0] // tm
    row = lambda i: (i, 0)
    tab = lambda i: (i % n_tab, 0)
    bf = jnp.bfloat16
    return pl.pallas_call(
        _inproj_kernel,
        grid=(t // tm,),
        in_specs=[
            pl.BlockSpec((tm, D_MODEL), row),
            _const_spec((1, D_MODEL)),
            _const_spec((D_MODEL, D_IN_EXT)),
            _const_spec((1, MLA_Q_RANK)),
            _const_spec((MLA_Q_RANK, MLA_QK)),
            _const_spec((1, MLA_KV_RANK)),
            _const_spec((MLA_KV_RANK, MLA_QK)),
            _const_spec((MLA_KV_RANK, MLA_QK)),
            pl.BlockSpec((tm, HEAD_GROUP), tab),
            pl.BlockSpec((tm, HEAD_GROUP), tab),
            pl.BlockSpec((tm, HEAD_GROUP), tab),
            pl.BlockSpec((tm, HEAD_GROUP), tab),
            _const_spec((1, MLA_QK)),
        ],
        out_specs=[
            pl.BlockSpec((tm, 3 * NA_W), row),
            pl.BlockSpec((tm, MLA_QK), row),
            pl.BlockSpec((tm, MLA_QK), row),
            pl.BlockSpec((MLA_QK, tm), lambda i: (0, i)),
            pl.BlockSpec((tm, 2 * D_MODEL), row),
        ],
        out_shape=[
            jax.ShapeDtypeStruct((t, 3 * NA_W), bf),
            jax.ShapeDtypeStruct((t, MLA_QK), bf),
            jax.ShapeDtypeStruct((t, MLA_QK), bf),
            jax.ShapeDtypeStruct((MLA_QK, t), bf),
            jax.ShapeDtypeStruct((t, 2 * D_MODEL), bf),
        ],
        compiler_params=pltpu.CompilerParams(
            dimension_semantics=("parallel",), vmem_limit_bytes=VMEM_LIMIT),
        name="inproj",
    )(x2, nmix, win, qn, wuq, kvn, wuk, wuv, cosq, sinq, cosk, sink, vone)


def _fold_lanes(x, op):
    out = x[:, :LANES]
    for t in range(1, x.shape[1] // LANES):
        out = op(out, x[:, t * LANES:(t + 1) * LANES])
    return out


def _na_kernel(q_ref, k_ref, v_ref, km_ref, vm_ref, tbl_ref, o_ref, s_ref, p_ref, *, n_blocks):
    bf = jnp.bfloat16
    f32 = jnp.float32
    nb = q_ref.shape[0] // NA_Q
    rows = n_blocks * NA_ROWS_PER_BLOCK
    units = [(i, hh) for i in range(nb) for hh in range(2)]
    n_units = len(units)
    lane = lax.broadcasted_iota(jnp.int32, (NA_Q, LANES), 1)
    in_head = (lane < NA_HEAD_DIM, lane >= NA_HEAD_DIM)

    def window(i):
        j = pl.program_id(2) * nb + i
        ws = jnp.clip(j * NA_ROWS_PER_BLOCK - NA_WIN_H // 2, 0, rows - NA_KEY_ROWS)
        variant = jnp.where(j == 0, 0, jnp.where(j == n_blocks - 1, 2, 1))
        return pl.multiple_of(ws * GRID_W, GRID_W), variant

    wins = [window(i) for i in range(nb)]

    def scores(u):
        i, hh = units[u]
        start, variant = wins[i]
        q2 = q_ref[i * NA_Q:(i + 1) * NA_Q, :]
        qh = jnp.where(in_head[hh], q2, jnp.zeros_like(q2))
        s_m = lax.dot_general(qh, km_ref[...], _NT, preferred_element_type=f32)
        s_m = s_m + tbl_ref[variant, hh, :, :META_PAD]
        s_g = lax.dot_general(qh, k_ref[pl.ds(start, NA_K), :], _NT, preferred_element_type=f32)
        s_g = s_g + tbl_ref[variant, hh, :, META_PAD:]
        s_ref[u % 2, :, :META_PAD] = s_m
        s_ref[u % 2, :, META_PAD:] = s_g
        return jnp.max(jnp.maximum(s_m, _fold_lanes(s_g, jnp.maximum)), axis=1, keepdims=True)

    def probs(u, m):
        p = jnp.exp2(s_ref[u % 2] - m)
        p_ref[u % 2] = p.astype(bf)
        return jnp.sum(_fold_lanes(p, jnp.add), axis=1, keepdims=True)

    def weighted(u):
        start = wins[units[u][0]][0]
        return (jnp.dot(p_ref[u % 2, :, :META_PAD], vm_ref[...], preferred_element_type=f32)
                + jnp.dot(p_ref[u % 2, :, META_PAD:], v_ref[pl.ds(start, NA_K), :], preferred_element_type=f32))

    row_max, denom, out = {}, {}, {}
    for k in range(n_units + 2):
        if k < n_units:
            row_max[k] = scores(k)
        if 1 <= k <= n_units:
            denom[k - 1] = probs(k - 1, row_max[k - 1])
        if k >= 2:
            out[k - 2] = weighted(k - 2) / denom[k - 2]
    for i in range(nb):
        o_ref[i * NA_Q:(i + 1) * NA_Q, :] = jnp.where(in_head[0], out[2 * i], out[2 * i + 1]).astype(bf)


def _na_attention(qkv, qkv_meta, tbl, batch, seq):
    n_blocks = seq // NA_Q
    n_steps = n_blocks // NA_BLOCKS_PER_STEP
    n_pairs = NA_HEADS // 2
    kcol = NA_W // LANES
    vcol = 2 * NA_W // LANES
    tq = NA_BLOCKS_PER_STEP * NA_Q
    nk = META_PAD + NA_K
    return pl.pallas_call(
        functools.partial(_na_kernel, n_blocks=n_blocks),
        grid=(n_pairs, batch, n_steps),
        in_specs=[
            pl.BlockSpec((tq, LANES), lambda hp, b, j: (b * n_steps + j, hp)),
            pl.BlockSpec((seq, LANES), lambda hp, b, j: (b, kcol + hp)),
            pl.BlockSpec((seq, LANES), lambda hp, b, j: (b, vcol + hp)),
            pl.BlockSpec((META_PAD, LANES), lambda hp, b, j: (0, kcol + hp)),
            pl.BlockSpec((META_PAD, LANES), lambda hp, b, j: (0, vcol + hp)),
            pl.BlockSpec((3, 2, NA_Q, nk), lambda hp, b, j: (0, hp, 0, 0)),
        ],
        out_specs=pl.BlockSpec((tq, LANES), lambda hp, b, j: (b * n_steps + j, hp)),
        out_shape=jax.ShapeDtypeStruct((batch * seq, NA_W), jnp.bfloat16),
        scratch_shapes=[pltpu.VMEM((2, NA_Q, nk), jnp.float32), pltpu.VMEM((2, NA_Q, nk), jnp.bfloat16)],
        compiler_params=pltpu.CompilerParams(
            dimension_semantics=("parallel", "parallel", "arbitrary"), vmem_limit_bytes=VMEM_LIMIT),
        name="na_attn",
    )(qkv, qkv, qkv, qkv_meta, qkv_meta, tbl)


def _na_bias_table(rpb, seq):
    rows = seq // GRID_W
    n_blocks = rows // NA_ROWS_PER_BLOCK
    kh = min(NA_WIN_H, rows)
    rpb = rpb.astype(jnp.float32) * LOG2E
    side = GRID_W - NA_WIN_W
    rpb_pad = jnp.pad(rpb, ((0, 0), (0, 0), (side, side)))
    toep = jnp.stack([rpb_pad[:, :, GRID_W - 1 - qc:2 * GRID_W - 1 - qc] for qc in range(GRID_W)], axis=2)
    qc = np.arange(GRID_W)[:, None]
    kc = np.arange(GRID_W)[None, :]
    cstart = np.clip(qc - NA_WIN_W // 2, 0, GRID_W - NA_WIN_W)
    cvalid = (kc >= cstart) & (kc < cstart + NA_WIN_W)
    toep = jnp.where(jnp.asarray(cvalid), toep, NEG)
    variants = []
    for j in (0, 1, n_blocks - 1):
        r0 = j * NA_ROWS_PER_BLOCK
        ws = int(np.clip(r0 - NA_WIN_H // 2, 0, rows - NA_KEY_ROWS))
        per_row = []
        for rq in range(NA_ROWS_PER_BLOCK):
            r = r0 + rq
            rs = int(np.clip(r - kh // 2, 0, rows - kh))
            dr_lo = rs - r + NA_WIN_H - 1
            rk_lo = rs - ws
            blk = toep[:, dr_lo:dr_lo + kh].transpose(0, 2, 1, 3)
            blk = jnp.pad(blk, ((0, 0), (0, 0), (rk_lo, NA_KEY_ROWS - rk_lo - kh), (0, 0)),
                          constant_values=NEG)
            per_row.append(blk.reshape(NA_HEADS, GRID_W, NA_K))
        variants.append(jnp.concatenate(per_row, axis=1))
    grid_part = jnp.stack(variants)
    meta_part = np.where(np.arange(META_PAD) < N_META, 0.0, NEG).astype(np.float32)
    meta_part = jnp.broadcast_to(jnp.asarray(meta_part), (3, NA_HEADS, NA_Q, META_PAD))
    return jnp.concatenate([meta_part, grid_part], axis=-1)


def _mla_kernel(q_ref, k_ref, vt_ref, km_ref, vtm_ref, o_ref, s_ref, p_ref):
    bf = jnp.bfloat16
    f32 = jnp.float32
    n_sub = q_ref.shape[0] // MLA_SUB
    n_chunks = k_ref.shape[0] // MLA_TK
    head = (slice(0, HEAD_GROUP), slice(HEAD_GROUP, 2 * HEAD_GROUP))
    chunk = [slice(c * MLA_TK, (c + 1) * MLA_TK) for c in range(n_chunks)]
    units = [(slice(i * MLA_SUB, (i + 1) * MLA_SUB), hh) for i in range(n_sub) for hh in range(2)]
    n_units = len(units)

    def meta_scores(u):
        rows, hh = units[u]
        return lax.dot_general(km_ref[:, head[hh]], q_ref[rows, head[hh]], _NT, preferred_element_type=f32)

    def scores(u, c):
        rows, hh = units[u]
        s = lax.dot_general(k_ref[chunk[c], head[hh]], q_ref[rows, head[hh]], _NT, preferred_element_type=f32)
        s_ref[u % 2, chunk[c], :] = s
        return jnp.max(s.reshape(MLA_TK // SUBLANES, SUBLANES, MLA_SUB), axis=0)

    def probs(u, c, m):
        p_ref[u % 2, chunk[c], :] = jnp.exp2(s_ref[u % 2, chunk[c], :] - m).astype(bf)

    def weighted(u, c):
        hh = units[u][1]
        return jnp.dot(vt_ref[head[hh], chunk[c]], p_ref[u % 2, chunk[c], :], preferred_element_type=f32)

    s_meta, col_max, acc = {}, {}, {}
    for k in range(n_units + 2):
        if k < n_units:
            s_meta[k] = meta_scores(k)
            mx = jnp.max(s_meta[k].reshape(N_META // SUBLANES, SUBLANES, MLA_SUB), axis=0)
        if k >= 2:
            p_meta = jnp.exp2(s_meta[k - 2] - col_max[k - 2]).astype(bf)
            acc[k - 2] = jnp.dot(vtm_ref[head[units[k - 2][1]], :], p_meta, preferred_element_type=f32)
        for c in range(n_chunks):
            if k < n_units:
                mx = jnp.maximum(mx, scores(k, c))
            if 1 <= k <= n_units:
                probs(k - 1, c, col_max[k - 1])
            if k >= 2:
                acc[k - 2] = acc[k - 2] + weighted(k - 2, c)
        if k < n_units:
            col_max[k] = jnp.max(mx, axis=0, keepdims=True)

    for i in range(n_sub):
        o0, o1 = [(acc[2 * i + hh] / acc[2 * i + hh][MLA_V_DIM:MLA_V_DIM + 1, :])[:MLA_V_DIM] for hh in range(2)]
        o_ref[units[2 * i][0], :] = jnp.concatenate([o0, o1], axis=0).T.astype(bf)


def _mla_attention(qm, km, vt, km_meta, vt_meta, batch, seq, tq):
    nq = seq // tq
    n_pairs = MLA_HEADS // 2
    pair = 2 * HEAD_GROUP
    return pl.pallas_call(
        _mla_kernel,
        grid=(batch, n_pairs, nq),
        in_specs=[
            pl.BlockSpec((tq, pair), lambda b, hp, i: (b * nq + i, hp)),
            pl.BlockSpec((seq, pair), lambda b, hp, i: (b, hp)),
            pl.BlockSpec((pair, seq), lambda b, hp, i: (hp, b)),
            pl.BlockSpec((N_META, pair), lambda b, hp, i: (0, hp)),
            pl.BlockSpec((pair, N_META), lambda b, hp, i: (hp, 0)),
        ],
        out_specs=pl.BlockSpec((tq, 2 * MLA_V_DIM), lambda b, hp, i: (b * nq + i, hp)),
        out_shape=jax.ShapeDtypeStruct((batch * seq, MLA_HEADS * MLA_V_DIM), jnp.bfloat16),
        scratch_shapes=[pltpu.VMEM((2, seq, MLA_SUB), jnp.float32), pltpu.VMEM((2, seq, MLA_SUB), jnp.bfloat16)],
        compiler_params=pltpu.CompilerParams(
            dimension_semantics=("parallel", "parallel", "arbitrary"), vmem_limit_bytes=VMEM_LIMIT),
        name="mla_attn",
    )(qm, km, vt, km_meta, vt_meta)


def _post_kernel(x_ref, ona_ref, omla_ref, g_ref, wna_ref, wmla_ref, wout_ref, nffn_ref,
                 wff1_ref, wff2_ref, nfin_ref, o_ref):
    bf = jnp.bfloat16
    f32 = jnp.float32
    a = jnp.dot(ona_ref[...], wna_ref[...], preferred_element_type=f32)
    b = jnp.dot(omla_ref[...], wmla_ref[...], preferred_element_type=f32)
    g_na = g_ref[:, :D_MODEL].astype(f32)
    g_mla = g_ref[:, D_MODEL:].astype(f32)
    merged = jax.nn.sigmoid(g_na) * a + jax.nn.sigmoid(g_mla) * b
    h = x_ref[...] + jnp.dot(merged.astype(bf), wout_ref[...], preferred_element_type=f32)
    fn = _rms(h, nffn_ref[...]).astype(bf)
    u = jnp.dot(fn, wff1_ref[...], preferred_element_type=f32)
    u = jnp.square(jnp.maximum(u, 0.0)).astype(bf)
    h = h + jnp.dot(u, wff2_ref[...], preferred_element_type=f32)
    o_ref[...] = _rms(h, nfin_ref[...])


def _post(x2, o_na, o_mla, g, wna, wmla, wout, nffn, wff1, wff2, nfin, tm):
    t = x2.shape[0]
    row = lambda i: (i, 0)
    return pl.pallas_call(
        _post_kernel,
        grid=(t // tm,),
        in_specs=[
            pl.BlockSpec((tm, D_MODEL), row),
            pl.BlockSpec((tm, NA_W), row),
            pl.BlockSpec((tm, MLA_HEADS * MLA_V_DIM), row),
            pl.BlockSpec((tm, 2 * D_MODEL), row),
            _const_spec((NA_W, D_MODEL)),
            _const_spec((MLA_HEADS * MLA_V_DIM, D_MODEL)),
            _const_spec((D_MODEL, D_MODEL)),
            _const_spec((1, D_MODEL)),
            _const_spec((D_MODEL, D_FF)),
            _const_spec((D_FF, D_MODEL)),
            _const_spec((1, D_MODEL)),
        ],
        out_specs=pl.BlockSpec((tm, D_MODEL), row),
        out_shape=jax.ShapeDtypeStruct((t, D_MODEL), jnp.float32),
        compiler_params=pltpu.CompilerParams(
            dimension_semantics=("parallel",), vmem_limit_bytes=VMEM_LIMIT),
        name="post",
    )(x2, o_na, o_mla, g, wna, wmla, wout, nffn, wff1, wff2, nfin)


def _head_groups(w, per_head, lane0):
    k = w.shape[0]
    w = w.reshape(k, MLA_HEADS, per_head)
    w = jnp.pad(w, ((0, 0), (0, 0), (lane0, HEAD_GROUP - lane0 - per_head)))
    return w.reshape(k, MLA_HEADS * HEAD_GROUP)


def _rope_tables(pos, scale):
    half = MLA_ROPE_DIM // 2
    inv_freq = 1.0 / (ROPE_THETA ** (jnp.arange(0, MLA_ROPE_DIM, 2, dtype=jnp.float32) / MLA_ROPE_DIM))
    ang = pos[:, None] * inv_freq[None, :]
    cos, sin = jnp.cos(ang), jnp.sin(ang)
    n = pos.shape[0]
    ones = jnp.ones((n, ROPE_LANE0), jnp.float32)
    tail = HEAD_GROUP - ROPE_LANE0 - MLA_ROPE_DIM
    cos_t = jnp.concatenate([ones, cos, cos, jnp.ones((n, tail), jnp.float32)], axis=1) * scale
    sin_t = jnp.concatenate([0 * ones, -sin, sin, jnp.zeros((n, tail), jnp.float32)], axis=1) * scale
    return cos_t, sin_t


def kernel(x, meta, norm_mix, w_in, na_rpb, mla_q_norm, w_uq, mla_kv_norm, w_ukv, w_na_out, w_mla_out,
           w_out, norm_ffn, w_ff1, w_ff2, norm_final):
    assert norm_mix.shape[0] == 1, "single-layer block"
    batch, seq, d = x.shape
    bf = jnp.bfloat16
    f32 = jnp.float32
    x2 = x.reshape(batch * seq, d)

    wi = w_in[0]
    kr_cols = jnp.pad(wi[:, C_KR:C_KR + MLA_ROPE_DIM],
                      ((0, 0), (ROPE_LANE0, HEAD_GROUP - ROPE_LANE0 - MLA_ROPE_DIM)))
    win = jnp.concatenate([wi[:, :C_KR], kr_cols, wi[:, C_KR + MLA_ROPE_DIM:]], axis=1).astype(bf)
    wuq = _head_groups(w_uq[0], MLA_NOPE_DIM + MLA_ROPE_DIM, 0).astype(bf)
    wukv = w_ukv[0].reshape(MLA_KV_RANK, MLA_HEADS, MLA_NOPE_DIM + MLA_V_DIM)
    wuk = _head_groups(wukv[:, :, :MLA_NOPE_DIM].reshape(MLA_KV_RANK, -1), MLA_NOPE_DIM, 0).astype(bf)
    wuv = _head_groups(wukv[:, :, MLA_NOPE_DIM:].reshape(MLA_KV_RANK, -1), MLA_V_DIM, 0).astype(bf)
    vone = (jnp.arange(MLA_QK) % HEAD_GROUP == MLA_V_DIM).astype(f32)[None]

    mla_scale = (MLA_NOPE_DIM + MLA_ROPE_DIM) ** -0.5 * LOG2E
    pos_tok = jnp.arange(N_META, N_META + seq, dtype=f32)
    pos_meta = jnp.arange(META_PAD, dtype=f32)
    cosq, sinq = _rope_tables(pos_tok, mla_scale)
    cosk, sink = _rope_tables(pos_tok, 1.0)
    cosq_m, sinq_m = _rope_tables(pos_meta, mla_scale)
    cosk_m, sink_m = _rope_tables(pos_meta, 1.0)

    small = (norm_mix[0][None], win, mla_q_norm[0][None], wuq, mla_kv_norm[0][None], wuk, wuv)
    qkv, qm, km, vt, g = _inproj(x2, *small, cosq, sinq, cosk, sink, vone, tm=512)
    meta_pad = jnp.pad(meta.astype(f32), ((0, META_PAD - N_META), (0, 0)))
    qkv_meta, _, km_meta, vt_meta, _ = _inproj(meta_pad, *small, cosq_m, sinq_m, cosk_m, sink_m, vone,
                                               tm=META_PAD)

    tbl = _na_bias_table(na_rpb[0], seq)
    o_na = _na_attention(qkv, qkv_meta, tbl, batch, seq)
    o_mla = _mla_attention(qm, km, vt, km_meta[:N_META], vt_meta[:, :N_META], batch, seq, tq=2048)

    out = _post(x2, o_na, o_mla, g, w_na_out[0].astype(bf), w_mla_out[0].astype(bf), w_out[0].astype(bf),
                norm_ffn[0][None], w_ff1[0].astype(bf), w_ff2[0].astype(bf), norm_final[None], tm=512)
    return out.reshape(batch, seq, d)
```

```python
import functools

import numpy as np
import jax
import jax.numpy as jnp
from jax import lax
from jax.experimental import pallas as pl
from jax.experimental.pallas import tpu as pltpu

D_MODEL = 1024
N_META = 16
GRID_W = 64
NA_HEADS = 8
NA_HEAD_DIM = 64
NA_WIN_H = 8
NA_WIN_W = 16
NA_W = NA_HEADS * NA_HEAD_DIM
MLA_HEADS = 8
MLA_NOPE_DIM = 64
MLA_ROPE_DIM = 32
MLA_V_DIM = 64
MLA_Q_RANK = 384
MLA_KV_RANK = 256
ROPE_THETA = 10000.0
D_FF = 4 * D_MODEL
EPS = 1e-6

LANES = 128
SUBLANES = 8
HEAD_GROUP = 128
ROPE_LANE0 = MLA_NOPE_DIM
MLA_QK = MLA_HEADS * HEAD_GROUP
NEG = -1e30

C_QKV = 0
C_CQ = 3 * NA_W
C_CKV = C_CQ + MLA_Q_RANK
C_KR = C_CKV + MLA_KV_RANK
C_G = C_KR + HEAD_GROUP
D_IN_EXT = C_G + 2 * D_MODEL

NA_ROWS_PER_BLOCK = 4
NA_KEY_ROWS = NA_ROWS_PER_BLOCK + NA_WIN_H
NA_Q = NA_ROWS_PER_BLOCK * GRID_W
NA_K = NA_KEY_ROWS * GRID_W
NA_BLOCKS_PER_STEP = 4
META_PAD = LANES
LOG2E = float(np.log2(np.e))
MLA_TK = 512
MLA_SUB = 512

VMEM_LIMIT = 56 * 1024 * 1024

_NT = (((1,), (1,)), ((), ()))


def _const_spec(shape):
    nd = len(shape)
    return pl.BlockSpec(shape, lambda *_: (0,) * nd, pipeline_mode=pl.Buffered(1))


def _rms(x, g):
    return x * lax.rsqrt(jnp.mean(x * x, axis=-1, keepdims=True) + EPS) * g


def _rope_group(x, cos, sin):
    lane = lax.broadcasted_iota(jnp.int32, x.shape, 1)
    partner = jnp.where(lane < ROPE_LANE0 + MLA_ROPE_DIM // 2,
                        pltpu.roll(x, HEAD_GROUP - MLA_ROPE_DIM // 2, 1),
                        pltpu.roll(x, MLA_ROPE_DIM // 2, 1))
    return x * cos + partner * sin


def _inproj_kernel(x_ref, nmix_ref, win_ref, qn_ref, wuq_ref, kvn_ref, wuk_ref, wuv_ref,
                   cosq_ref, sinq_ref, cosk_ref, sink_ref, vone_ref,
                   qkv_ref, qm_ref, km_ref, vt_ref, g_ref):
    bf = jnp.bfloat16
    f32 = jnp.float32
    hn = _rms(x_ref[...], nmix_ref[...]).astype(bf)

    qkv = jnp.dot(hn, win_ref[:, C_QKV:C_CQ], preferred_element_type=f32)
    qkv_ref[:, :NA_W] = (qkv[:, :NA_W] * (NA_HEAD_DIM ** -0.5 * LOG2E)).astype(bf)
    qkv_ref[:, NA_W:] = qkv[:, NA_W:].astype(bf)

    g_ref[...] = jnp.dot(hn, win_ref[:, C_G:], preferred_element_type=f32).astype(bf)

    lat = jnp.dot(hn, win_ref[:, C_CQ:C_G], preferred_element_type=f32)
    cq = _rms(lat[:, :MLA_Q_RANK], qn_ref[...]).astype(bf)
    ckv = _rms(lat[:, MLA_Q_RANK:MLA_Q_RANK + MLA_KV_RANK], kvn_ref[...]).astype(bf)
    kr = lat[:, MLA_Q_RANK + MLA_KV_RANK:]
    kr = _rope_group(kr, cosk_ref[...], sink_ref[...])

    q = jnp.dot(cq, wuq_ref[...], preferred_element_type=f32)
    kn = jnp.dot(ckv, wuk_ref[...], preferred_element_type=f32)
    vv = jnp.dot(ckv, wuv_ref[...], preferred_element_type=f32)
    cosq = cosq_ref[...]
    sinq = sinq_ref[...]
    for h in range(MLA_HEADS):
        sl = slice(h * HEAD_GROUP, (h + 1) * HEAD_GROUP)
        qm_ref[:, sl] = _rope_group(q[:, sl], cosq, sinq).astype(bf)
        km_ref[:, sl] = (kn[:, sl] + kr).astype(bf)
    vt_ref[...] = (vv + vone_ref[...]).T.astype(bf)


def _inproj(x2, nmix, win, qn, wuq, kvn, wuk, wuv, cosq, sinq, cosk, sink, vone, tm):
    t = x2.shape[0]
    n_tab = cosq.shape[0] // tm
    row = lambda i: (i, 0)
    tab = lambda i: (i % n_tab, 0)
    bf = jnp.bfloat16
    return pl.pallas_call(
        _inproj_kernel,
        grid=(t // tm,),
        in_specs=[
            pl.BlockSpec((tm, D_MODEL), row),
            _const_spec((1, D_MODEL)),
            _const_spec((D_MODEL, D_IN_EXT)),
            _const_spec((1, MLA_Q_RANK)),
            _const_spec((MLA_Q_RANK, MLA_QK)),
            _const_spec((1, MLA_KV_RANK)),
            _const_spec((MLA_KV_RANK, MLA_QK)),
            _const_spec((MLA_KV_RANK, MLA_QK)),
            pl.BlockSpec((tm, HEAD_GROUP), tab),
            pl.BlockSpec((tm, HEAD_GROUP), tab),
            pl.BlockSpec((tm, HEAD_GROUP), tab),
            pl.BlockSpec((tm, HEAD_GROUP), tab),
            _const_spec((1, MLA_QK)),
        ],
        out_specs=[
            pl.BlockSpec((tm, 3 * NA_W), row),
            pl.BlockSpec((tm, MLA_QK), row),
            pl.BlockSpec((tm, MLA_QK), row),
            pl.BlockSpec((MLA_QK, tm), lambda i: (0, i)),
            pl.BlockSpec((tm, 2 * D_MODEL), row),
        ],
        out_shape=[
            jax.ShapeDtypeStruct((t, 3 * NA_W), bf),
            jax.ShapeDtypeStruct((t, MLA_QK), bf),
            jax.ShapeDtypeStruct((t, MLA_QK), bf),
            jax.ShapeDtypeStruct((MLA_QK, t), bf),
            jax.ShapeDtypeStruct((t, 2 * D_MODEL), bf),
        ],
        compiler_params=pltpu.CompilerParams(
            dimension_semantics=("parallel",), vmem_limit_bytes=VMEM_LIMIT),
        name="inproj",
    )(x2, nmix, win, qn, wuq, kvn, wuk, wuv, cosq, sinq, cosk, sink, vone)


def _fold_lanes(x, op):
    out = x[:, :LANES]
    for t in range(1, x.shape[1] // LANES):
        out = op(out, x[:, t * LANES:(t + 1) * LANES])
    return out


def _na_kernel(q_ref, k_ref, v_ref, km_ref, vm_ref, tbl_ref, o_ref, s_ref, p_ref, *, n_blocks):
    bf = jnp.bfloat16
    f32 = jnp.float32
    nb = q_ref.shape[0] // NA_Q
    rows = n_blocks * NA_ROWS_PER_BLOCK
    units = [(i, hh) for i in range(nb) for hh in range(2)]
    n_units = len(units)
    lane = lax.broadcasted_iota(jnp.int32, (NA_Q, LANES), 1)
    in_head = (lane < NA_HEAD_DIM, lane >= NA_HEAD_DIM)

    def window(i):
        j = pl.program_id(2) * nb + i
        ws = jnp.clip(j * NA_ROWS_PER_BLOCK - NA_WIN_H // 2, 0, rows - NA_KEY_ROWS)
        variant = jnp.where(j == 0, 0, jnp.where(j == n_blocks - 1, 2, 1))
        return pl.multiple_of(ws * GRID_W, GRID_W), variant

    wins = [window(i) for i in range(nb)]

    def scores(u):
        i, hh = units[u]
        start, variant = wins[i]
        q2 = q_ref[i * NA_Q:(i + 1) * NA_Q, :]
        qh = jnp.where(in_head[hh], q2, jnp.zeros_like(q2))
        s_m = lax.dot_general(qh, km_ref[...], _NT, preferred_element_type=f32)
        s_m = s_m + tbl_ref[variant, hh, :, :META_PAD]
        s_g = lax.dot_general(qh, k_ref[pl.ds(start, NA_K), :], _NT, preferred_element_type=f32)
        s_g = s_g + tbl_ref[variant, hh, :, META_PAD:]
        s_ref[u % 2, :, :META_PAD] = s_m
        s_ref[u % 2, :, META_PAD:] = s_g
        return jnp.max(jnp.maximum(s_m, _fold_lanes(s_g, jnp.maximum)), axis=1, keepdims=True)

    def probs(u, m):
        p = jnp.exp2(s_ref[u % 2] - m)
        p_ref[u % 2] = p.astype(bf)
        return jnp.sum(_fold_lanes(p, jnp.add), axis=1, keepdims=True)

    def weighted(u):
        start = wins[units[u][0]][0]
        return (jnp.dot(p_ref[u % 2, :, :META_PAD], vm_ref[...], preferred_element_type=f32)
                + jnp.dot(p_ref[u % 2, :, META_PAD:], v_ref[pl.ds(start, NA_K), :], preferred_element_type=f32))

    row_max, denom, out = {}, {}, {}
    for k in range(n_units + 2):
        if k < n_units:
            row_max[k] = scores(k)
        if 1 <= k <= n_units:
            denom[k - 1] = probs(k - 1, row_max[k - 1])
        if k >= 2:
            out[k - 2] = weighted(k - 2) / denom[k - 2]
    for i in range(nb):
        o_ref[i * NA_Q:(i + 1) * NA_Q, :] = jnp.where(in_head[0], out[2 * i], out[2 * i + 1]).astype(bf)


def _na_attention(qkv, qkv_meta, tbl, batch, seq):
    n_blocks = seq // NA_Q
    n_steps = n_blocks // NA_BLOCKS_PER_STEP
    n_pairs = NA_HEADS // 2
    kcol = NA_W // LANES
    vcol = 2 * NA_W // LANES
    tq = NA_BLOCKS_PER_STEP * NA_Q
    nk = META_PAD + NA_K
    return pl.pallas_call(
        functools.partial(_na_kernel, n_blocks=n_blocks),
        grid=(n_pairs, batch, n_steps),
        in_specs=[
            pl.BlockSpec((tq, LANES), lambda hp, b, j: (b * n_steps + j, hp)),
            pl.BlockSpec((seq, LANES), lambda hp, b, j: (b, kcol + hp)),
            pl.BlockSpec((seq, LANES), lambda hp, b, j: (b, vcol + hp)),
            pl.BlockSpec((META_PAD, LANES), lambda hp, b, j: (0, kcol + hp)),
            pl.BlockSpec((META_PAD, LANES), lambda hp, b, j: (0, vcol + hp)),
            pl.BlockSpec((3, 2, NA_Q, nk), lambda hp, b, j: (0, hp, 0, 0)),
        ],
        out_specs=pl.BlockSpec((tq, LANES), lambda hp, b, j: (b * n_steps + j, hp)),
        out_shape=jax.ShapeDtypeStruct((batch * seq, NA_W), jnp.bfloat16),
        scratch_shapes=[pltpu.VMEM((2, NA_Q, nk), jnp.float32), pltpu.VMEM((2, NA_Q, nk), jnp.bfloat16)],
        compiler_params=pltpu.CompilerParams(
            dimension_semantics=("parallel", "parallel", "arbitrary"), vmem_limit_bytes=VMEM_LIMIT),
        name="na_attn",
    )(qkv, qkv, qkv, qkv_meta, qkv_meta, tbl)


def _na_bias_kernel(w_ref, o_ref, *, rows, n_blocks):
    kh = min(NA_WIN_H, rows)
    shape = (GRID_W, LANES)
    lane = lax.broadcasted_iota(jnp.int32, shape, 1)
    qc = lax.broadcasted_iota(jnp.int32, shape, 0)
    kc = lane % GRID_W
    cstart = jnp.clip(qc - NA_WIN_W // 2, 0, GRID_W - NA_WIN_W)
    cvalid = (kc >= cstart) & (kc < cstart + NA_WIN_W)
    neg = jnp.full(shape, NEG, jnp.float32)
    halves = {}

    def toeplitz(dr, half):
        if (dr, half) not in halves:
            row = jnp.broadcast_to(w_ref[0, dr:dr + 1, :], shape)
            halves[dr, half] = pltpu.roll(row, half * GRID_W, 1, stride=1, stride_axis=0)
        return halves[dr, half]

    meta = jnp.where(lax.broadcasted_iota(jnp.int32, (NA_Q, META_PAD), 1) < N_META, 0.0, NEG)
    for variant, j in enumerate((0, 1, n_blocks - 1)):
        r0 = j * NA_ROWS_PER_BLOCK
        ws = min(max(r0 - NA_WIN_H // 2, 0), rows - NA_KEY_ROWS)
        o_ref[variant, 0, :, :META_PAD] = meta
        for rq in range(NA_ROWS_PER_BLOCK):
            r = r0 + rq
            rs = min(max(r - kh // 2, 0), rows - kh)
            for pair in range(NA_KEY_ROWS // 2):
                tiles = []
                for half in range(2):
                    krow = ws + 2 * pair + half
                    ok = rs <= krow < rs + kh
                    tiles.append(toeplitz(krow - r + NA_WIN_H - 1, half) if ok else neg)
                tile = jnp.where(cvalid, jnp.where(lane < GRID_W, tiles[0], tiles[1]), NEG)
                o_ref[variant, 0, rq * GRID_W:(rq + 1) * GRID_W,
                      META_PAD + pair * LANES:META_PAD + (pair + 1) * LANES] = tile


def _na_bias_table(rpb, seq):
    rows = seq // GRID_W
    n_blocks = rows // NA_ROWS_PER_BLOCK
    n_dr = 2 * NA_WIN_H - 1
    rpb = rpb.astype(jnp.float32) * LOG2E
    side = GRID_W - NA_WIN_W
    rpb_pad = jnp.pad(rpb, ((0, 0), (0, 0), (side, side)))
    w = jnp.concatenate([rpb_pad[..., GRID_W - 1:], jnp.zeros(rpb.shape[:2] + (1,), jnp.float32),
                         rpb_pad[..., :GRID_W - 1]], axis=-1)
    nk = META_PAD + NA_K
    return pl.pallas_call(
        functools.partial(_na_bias_kernel, rows=rows, n_blocks=n_blocks),
        grid=(NA_HEADS,),
        in_specs=[pl.BlockSpec((1, n_dr, LANES), lambda h: (h, 0, 0))],
        out_specs=pl.BlockSpec((3, 1, NA_Q, nk), lambda h: (0, h, 0, 0)),
        out_shape=jax.ShapeDtypeStruct((3, NA_HEADS, NA_Q, nk), jnp.float32),
        compiler_params=pltpu.CompilerParams(dimension_semantics=("parallel",), vmem_limit_bytes=VMEM_LIMIT),
        name="na_bias",
    )(w)


def _mla_kernel(q_ref, k_ref, vt_ref, km_ref, vtm_ref, o_ref, s_ref, p_ref):
    bf = jnp.bfloat16
    f32 = jnp.float32
    n_sub = q_ref.shape[0] // MLA_SUB
    n_chunks = k_ref.shape[0] // MLA_TK
    head = (slice(0, HEAD_GROUP), slice(HEAD_GROUP, 2 * HEAD_GROUP))
    chunk = [slice(c * MLA_TK, (c + 1) * MLA_TK) for c in range(n_chunks)]
    units = [(slice(i * MLA_SUB, (i + 1) * MLA_SUB), hh) for i in range(n_sub) for hh in range(2)]
    n_units = len(units)

    def meta_scores(u):
        rows, hh = units[u]
        return lax.dot_general(km_ref[:, head[hh]], q_ref[rows, head[hh]], _NT, preferred_element_type=f32)

    def scores(u, c):
        rows, hh = units[u]
        s = lax.dot_general(k_ref[chunk[c], head[hh]], q_ref[rows, head[hh]], _NT, preferred_element_type=f32)
        s_ref[u % 2, chunk[c], :] = s
        return jnp.max(s.reshape(MLA_TK // SUBLANES, SUBLANES, MLA_SUB), axis=0)

    def probs(u, c, m):
        p_ref[u % 2, chunk[c], :] = jnp.exp2(s_ref[u % 2, chunk[c], :] - m).astype(bf)

    def weighted(u, c):
        hh = units[u][1]
        return jnp.dot(vt_ref[head[hh], chunk[c]], p_ref[u % 2, chunk[c], :], preferred_element_type=f32)

    s_meta, col_max, acc = {}, {}, {}
    for k in range(n_units + 2):
        if k < n_units:
            s_meta[k] = meta_scores(k)
            mx = jnp.max(s_meta[k].reshape(N_META // SUBLANES, SUBLANES, MLA_SUB), axis=0)
        if k >= 2:
            p_meta = jnp.exp2(s_meta[k - 2] - col_max[k - 2]).astype(bf)
            acc[k - 2] = jnp.dot(vtm_ref[head[units[k - 2][1]], :], p_meta, preferred_element_type=f32)
        for c in range(n_chunks):
            if k < n_units:
                mx = jnp.maximum(mx, scores(k, c))
            if 1 <= k <= n_units:
                probs(k - 1, c, col_max[k - 1])
            if k >= 2:
                acc[k - 2] = acc[k - 2] + weighted(k - 2, c)
        if k < n_units:
            col_max[k] = jnp.max(mx, axis=0, keepdims=True)

    for i in range(n_sub):
        o0, o1 = [(acc[2 * i + hh] / acc[2 * i + hh][MLA_V_DIM:MLA_V_DIM + 1, :])[:MLA_V_DIM] for hh in range(2)]
        o_ref[units[2 * i][0], :] = jnp.concatenate([o0, o1], axis=0).T.astype(bf)


def _mla_attention(qm, km, vt, km_meta, vt_meta, batch, seq, tq):
    nq = seq // tq
    n_pairs = MLA_HEADS // 2
    pair = 2 * HEAD_GROUP
    return pl.pallas_call(
        _mla_kernel,
        grid=(batch, n_pairs, nq),
        in_specs=[
            pl.BlockSpec((tq, pair), lambda b, hp, i: (b * nq + i, hp)),
            pl.BlockSpec((seq, pair), lambda b, hp, i: (b, hp)),
            pl.BlockSpec((pair, seq), lambda b, hp, i: (hp, b)),
            pl.BlockSpec((N_META, pair), lambda b, hp, i: (0, hp)),
            pl.BlockSpec((pair, N_META), lambda b, hp, i: (hp, 0)),
        ],
        out_specs=pl.BlockSpec((tq, 2 * MLA_V_DIM), lambda b, hp, i: (b * nq + i, hp)),
        out_shape=jax.ShapeDtypeStruct((batch * seq, MLA_HEADS * MLA_V_DIM), jnp.bfloat16),
        scratch_shapes=[pltpu.VMEM((2, seq, MLA_SUB), jnp.float32), pltpu.VMEM((2, seq, MLA_SUB), jnp.bfloat16)],
        compiler_params=pltpu.CompilerParams(
            dimension_semantics=("parallel", "parallel", "arbitrary"), vmem_limit_bytes=VMEM_LIMIT),
        name="mla_attn",
    )(qm, km, vt, km_meta, vt_meta)


def _post_kernel(x_ref, ona_ref, omla_ref, g_ref, wna_ref, wmla_ref, wout_ref, nffn_ref,
                 wff1_ref, wff2_ref, nfin_ref, o_ref):
    bf = jnp.bfloat16
    f32 = jnp.float32
    a = jnp.dot(ona_ref[...], wna_ref[...], preferred_element_type=f32)
    b = jnp.dot(omla_ref[...], wmla_ref[...], preferred_element_type=f32)
    g_na = g_ref[:, :D_MODEL].astype(f32)
    g_mla = g_ref[:, D_MODEL:].astype(f32)
    merged = jax.nn.sigmoid(g_na) * a + jax.nn.sigmoid(g_mla) * b
    h = x_ref[...] + jnp.dot(merged.astype(bf), wout_ref[...], preferred_element_type=f32)
    fn = _rms(h, nffn_ref[...]).astype(bf)
    u = jnp.dot(fn, wff1_ref[...], preferred_element_type=f32)
    u = jnp.square(jnp.maximum(u, 0.0)).astype(bf)
    h = h + jnp.dot(u, wff2_ref[...], preferred_element_type=f32)
    o_ref[...] = _rms(h, nfin_ref[...])


def _post(x2, o_na, o_mla, g, wna, wmla, wout, nffn, wff1, wff2, nfin, tm):
    t = x2.shape[0]
    row = lambda i: (i, 0)
    return pl.pallas_call(
        _post_kernel,
        grid=(t // tm,),
        in_specs=[
            pl.BlockSpec((tm, D_MODEL), row),
            pl.BlockSpec((tm, NA_W), row),
            pl.BlockSpec((tm, MLA_HEADS * MLA_V_DIM), row),
            pl.BlockSpec((tm, 2 * D_MODEL), row),
            _const_spec((NA_W, D_MODEL)),
            _const_spec((MLA_HEADS * MLA_V_DIM, D_MODEL)),
            _const_spec((D_MODEL, D_MODEL)),
            _const_spec((1, D_MODEL)),
            _const_spec((D_MODEL, D_FF)),
            _const_spec((D_FF, D_MODEL)),
            _const_spec((1, D_MODEL)),
        ],
        out_specs=pl.BlockSpec((tm, D_MODEL), row),
        out_shape=jax.ShapeDtypeStruct((t, D_MODEL), jnp.float32),
        compiler_params=pltpu.CompilerParams(
            dimension_semantics=("parallel",), vmem_limit_bytes=VMEM_LIMIT),
        name="post",
    )(x2, o_na, o_mla, g, wna, wmla, wout, nffn, wff1, wff2, nfin)


def _head_groups(w, per_head, lane0):
    k = w.shape[0]
    w = w.reshape(k, MLA_HEADS, per_head)
    w = jnp.pad(w, ((0, 0), (0, 0), (lane0, HEAD_GROUP - lane0 - per_head)))
    return w.reshape(k, MLA_HEADS * HEAD_GROUP)


def _rope_tables(pos, scale):
    half = MLA_ROPE_DIM // 2
    inv_freq = 1.0 / (ROPE_THETA ** (jnp.arange(0, MLA_ROPE_DIM, 2, dtype=jnp.float32) / MLA_ROPE_DIM))
    ang = pos[:, None] * inv_freq[None, :]
    cos, sin = jnp.cos(ang), jnp.sin(ang)
    n = pos.shape[0]
    ones = jnp.ones((n, ROPE_LANE0), jnp.float32)
    tail = HEAD_GROUP - ROPE_LANE0 - MLA_ROPE_DIM
    cos_t = jnp.concatenate([ones, cos, cos, jnp.ones((n, tail), jnp.float32)], axis=1) * scale
    sin_t = jnp.concatenate([0 * ones, -sin, sin, jnp.zeros((n, tail), jnp.float32)], axis=1) * scale
    return cos_t, sin_t


def kernel(x, meta, norm_mix, w_in, na_rpb, mla_q_norm, w_uq, mla_kv_norm, w_ukv, w_na_out, w_mla_out,
           w_out, norm_ffn, w_ff1, w_ff2, norm_final):
    assert norm_mix.shape[0] == 1, "single-layer block"
    batch, seq, d = x.shape
    bf = jnp.bfloat16
    f32 = jnp.float32
    x2 = x.reshape(batch * seq, d)

    wi = w_in[0]
    kr_cols = jnp.pad(wi[:, C_KR:C_KR + MLA_ROPE_DIM],
                      ((0, 0), (ROPE_LANE0, HEAD_GROUP - ROPE_LANE0 - MLA_ROPE_DIM)))
    win = jnp.concatenate([wi[:, :C_KR], kr_cols, wi[:, C_KR + MLA_ROPE_DIM:]], axis=1).astype(bf)
    wuq = _head_groups(w_uq[0], MLA_NOPE_DIM + MLA_ROPE_DIM, 0).astype(bf)
    wukv = w_ukv[0].reshape(MLA_KV_RANK, MLA_HEADS, MLA_NOPE_DIM + MLA_V_DIM)
    wuk = _head_groups(wukv[:, :, :MLA_NOPE_DIM].reshape(MLA_KV_RANK, -1), MLA_NOPE_DIM, 0).astype(bf)
    wuv = _head_groups(wukv[:, :, MLA_NOPE_DIM:].reshape(MLA_KV_RANK, -1), MLA_V_DIM, 0).astype(bf)
    vone = (jnp.arange(MLA_QK) % HEAD_GROUP == MLA_V_DIM).astype(f32)[None]

    mla_scale = (MLA_NOPE_DIM + MLA_ROPE_DIM) ** -0.5 * LOG2E
    pos_tok = jnp.arange(N_META, N_META + seq, dtype=f32)
    pos_meta = jnp.arange(META_PAD, dtype=f32)
    cosq, sinq = _rope_tables(pos_tok, mla_scale)
    cosk, sink = _rope_tables(pos_tok, 1.0)
    cosq_m, sinq_m = _rope_tables(pos_meta, mla_scale)
    cosk_m, sink_m = _rope_tables(pos_meta, 1.0)

    small = (norm_mix[0][None], win, mla_q_norm[0][None], wuq, mla_kv_norm[0][None], wuk, wuv)
    qkv, qm, km, vt, g = _inproj(x2, *small, cosq, sinq, cosk, sink, vone, tm=512)
    meta_pad = jnp.pad(meta.astype(f32), ((0, META_PAD - N_META), (0, 0)))
    qkv_meta, _, km_meta, vt_meta, _ = _inproj(meta_pad, *small, cosq_m, sinq_m, cosk_m, sink_m, vone,
                                               tm=META_PAD)

    tbl = _na_bias_table(na_rpb[0], seq)
    o_na = _na_attention(qkv, qkv_meta, tbl, batch, seq)
    o_mla = _mla_attention(qm, km, vt, km_meta[:N_META], vt_meta[:, :N_META], batch, seq, tq=2048)

    out = _post(x2, o_na, o_mla, g, w_na_out[0].astype(bf), w_mla_out[0].astype(bf), w_out[0].astype(bf),
                norm_ffn[0][None], w_ff1[0].astype(bf), w_ff2[0].astype(bf), norm_final[None], tm=512)
    return out.reshape(batch, seq, d)
```

```python
import functools

import numpy as np
import jax
import jax.numpy as jnp
from jax import lax
from jax.experimental import pallas as pl
from jax.experimental.pallas import tpu as pltpu

D_MODEL = 1024
N_META = 16
GRID_W = 64
NA_HEADS = 8
NA_HEAD_DIM = 64
NA_WIN_H = 8
NA_WIN_W = 16
NA_W = NA_HEADS * NA_HEAD_DIM
MLA_HEADS = 8
MLA_NOPE_DIM = 64
MLA_ROPE_DIM = 32
MLA_V_DIM = 64
MLA_Q_RANK = 384
MLA_KV_RANK = 256
ROPE_THETA = 10000.0
D_FF = 4 * D_MODEL
EPS = 1e-6

LANES = 128
SUBLANES = 8
HEAD_GROUP = 128
ROPE_LANE0 = MLA_NOPE_DIM
MLA_QK = MLA_HEADS * HEAD_GROUP
NEG = -1e30

C_QKV = 0
C_CQ = 3 * NA_W
C_CKV = C_CQ + MLA_Q_RANK
C_KR = C_CKV + MLA_KV_RANK
C_G = C_KR + HEAD_GROUP
D_IN_EXT = C_G + 2 * D_MODEL

NA_ROWS_PER_BLOCK = 4
NA_KEY_ROWS = NA_ROWS_PER_BLOCK + NA_WIN_H
NA_Q = NA_ROWS_PER_BLOCK * GRID_W
NA_K = NA_KEY_ROWS * GRID_W
NA_KT = NA_K + N_META
NA_BLOCKS_PER_STEP = 16
META_PAD = LANES
LOG2E = float(np.log2(np.e))
MLA_TK = 512
MLA_SUB = 512

VMEM_LIMIT = 56 * 1024 * 1024

_NT = (((1,), (1,)), ((), ()))


def _const_spec(shape):
    nd = len(shape)
    return pl.BlockSpec(shape, lambda *_: (0,) * nd, pipeline_mode=pl.Buffered(1))


def _rms(x, g):
    return x * lax.rsqrt(jnp.mean(x * x, axis=-1, keepdims=True) + EPS) * g


def _rope_group(x, cos, sin):
    lane = lax.broadcasted_iota(jnp.int32, x.shape, 1)
    partner = jnp.where(lane < ROPE_LANE0 + MLA_ROPE_DIM // 2,
                        pltpu.roll(x, HEAD_GROUP - MLA_ROPE_DIM // 2, 1),
                        pltpu.roll(x, MLA_ROPE_DIM // 2, 1))
    return x * cos + partner * sin


def _fold_rows(x, op):
    return op(x.reshape(x.shape[0] // SUBLANES, SUBLANES, x.shape[1]), axis=0)


def _inproj_kernel(x_ref, nmix_ref, win_ref, qn_ref, wuq_ref, kvn_ref, wuk_ref, wuv_ref,
                   cosq_ref, sinq_ref, cosk_ref, sink_ref, vone_ref,
                   qk_ref, vnat_ref, qm_ref, km_ref, vt_ref, g_ref):
    bf = jnp.bfloat16
    f32 = jnp.float32
    hn = _rms(x_ref[...], nmix_ref[...]).astype(bf)

    qkv = jnp.dot(hn, win_ref[:, C_QKV:C_CQ], preferred_element_type=f32)
    qk_ref[:, :NA_W] = (qkv[:, :NA_W] * (NA_HEAD_DIM ** -0.5 * LOG2E)).astype(bf)
    qk_ref[:, NA_W:] = qkv[:, NA_W:2 * NA_W].astype(bf)
    vnat_ref[...] = qkv[:, 2 * NA_W:].T.astype(bf)

    g_ref[...] = jnp.dot(hn, win_ref[:, C_G:], preferred_element_type=f32).astype(bf)

    lat = jnp.dot(hn, win_ref[:, C_CQ:C_G], preferred_element_type=f32)
    cq = _rms(lat[:, :MLA_Q_RANK], qn_ref[...]).astype(bf)
    ckv = _rms(lat[:, MLA_Q_RANK:MLA_Q_RANK + MLA_KV_RANK], kvn_ref[...]).astype(bf)
    kr = lat[:, MLA_Q_RANK + MLA_KV_RANK:]
    kr = _rope_group(kr, cosk_ref[...], sink_ref[...])

    q = jnp.dot(cq, wuq_ref[...], preferred_element_type=f32)
    kn = jnp.dot(ckv, wuk_ref[...], preferred_element_type=f32)
    vv = jnp.dot(ckv, wuv_ref[...], preferred_element_type=f32)
    cosq = cosq_ref[...]
    sinq = sinq_ref[...]
    for h in range(MLA_HEADS):
        sl = slice(h * HEAD_GROUP, (h + 1) * HEAD_GROUP)
        qm_ref[:, sl] = _rope_group(q[:, sl], cosq, sinq).astype(bf)
        km_ref[:, sl] = (kn[:, sl] + kr).astype(bf)
    vt_ref[...] = (vv + vone_ref[...]).T.astype(bf)


def _inproj(x2, nmix, win, qn, wuq, kvn, wuk, wuv, cosq, sinq, cosk, sink, vone, tm):
    t = x2.shape[0]
    n_tab = cosq.shape[0] // tm
    row = lambda i: (i, 0)
    col = lambda i: (0, i)
    tab = lambda i: (i % n_tab, 0)
    bf = jnp.bfloat16
    return pl.pallas_call(
        _inproj_kernel,
        grid=(t // tm,),
        in_specs=[
            pl.BlockSpec((tm, D_MODEL), row),
            _const_spec((1, D_MODEL)),
            _const_spec((D_MODEL, D_IN_EXT)),
            _const_spec((1, MLA_Q_RANK)),
            _const_spec((MLA_Q_RANK, MLA_QK)),
            _const_spec((1, MLA_KV_RANK)),
            _const_spec((MLA_KV_RANK, MLA_QK)),
            _const_spec((MLA_KV_RANK, MLA_QK)),
            pl.BlockSpec((tm, HEAD_GROUP), tab),
            pl.BlockSpec((tm, HEAD_GROUP), tab),
            pl.BlockSpec((tm, HEAD_GROUP), tab),
            pl.BlockSpec((tm, HEAD_GROUP), tab),
            _const_spec((1, MLA_QK)),
        ],
        out_specs=[
            pl.BlockSpec((tm, 2 * NA_W), row),
            pl.BlockSpec((NA_W, tm), col),
            pl.BlockSpec((tm, MLA_QK), row),
            pl.BlockSpec((tm, MLA_QK), row),
            pl.BlockSpec((MLA_QK, tm), col),
            pl.BlockSpec((tm, 2 * D_MODEL), row),
        ],
        out_shape=[
            jax.ShapeDtypeStruct((t, 2 * NA_W), bf),
            jax.ShapeDtypeStruct((NA_W, t), bf),
            jax.ShapeDtypeStruct((t, MLA_QK), bf),
            jax.ShapeDtypeStruct((t, MLA_QK), bf),
            jax.ShapeDtypeStruct((MLA_QK, t), bf),
            jax.ShapeDtypeStruct((t, 2 * D_MODEL), bf),
        ],
        compiler_params=pltpu.CompilerParams(
            dimension_semantics=("parallel",), vmem_limit_bytes=VMEM_LIMIT),
        name="inproj",
    )(x2, nmix, win, qn, wuq, kvn, wuk, wuv, cosq, sinq, cosk, sink, vone)


def _na_kernel(q_ref, k_ref, vt_ref, km_ref, vtm_ref, tbl_ref, o_ref, s_ref, p_ref, *, n_blocks):
    bf = jnp.bfloat16
    f32 = jnp.float32
    nb = q_ref.shape[0] // NA_Q
    rows = n_blocks * NA_ROWS_PER_BLOCK
    lane = lax.broadcasted_iota(jnp.int32, (NA_Q, LANES), 1)
    in_head = (lane < NA_HEAD_DIM, lane >= NA_HEAD_DIM)

    def window(i):
        j = pl.program_id(2) * nb + i
        ws = jnp.clip(j * NA_ROWS_PER_BLOCK - NA_WIN_H // 2, 0, rows - NA_KEY_ROWS)
        variant = jnp.where(j == 0, 0, jnp.where(j == n_blocks - 1, 2, 1))
        return pl.multiple_of(ws * GRID_W, NA_ROWS_PER_BLOCK * GRID_W), variant

    wins = [window(i) for i in range(nb)]

    def scores(u):
        start, variant = wins[u]
        q2 = q_ref[u * NA_Q:(u + 1) * NA_Q, :]
        zero = jnp.zeros_like(q2)
        qq = jnp.concatenate([jnp.where(in_head[0], q2, zero), jnp.where(in_head[1], q2, zero)], axis=0)
        s_g = lax.dot_general(k_ref[pl.ds(start, NA_K), :], qq, _NT, preferred_element_type=f32)
        s_g = s_g + tbl_ref[variant, 0, :NA_K, :]
        s_m = lax.dot_general(km_ref[...], qq, _NT, preferred_element_type=f32)
        s_m = s_m + tbl_ref[variant, 0, NA_K:, :]
        s_ref[u % 2, :NA_K, :] = s_g
        s_ref[u % 2, NA_K:, :] = s_m
        mx = jnp.maximum(_fold_rows(s_g, jnp.max), _fold_rows(s_m, jnp.max))
        return jnp.max(mx, axis=0, keepdims=True)

    def probs(u, m):
        p = jnp.exp2(s_ref[u % 2] - m)
        p_ref[u % 2] = p.astype(bf)
        return jnp.sum(_fold_rows(p, jnp.sum), axis=0, keepdims=True)

    def weighted(u):
        start = wins[u][0]
        return (jnp.dot(vt_ref[:, pl.ds(start, NA_K)], p_ref[u % 2, :NA_K, :], preferred_element_type=f32)
                + jnp.dot(vtm_ref[...], p_ref[u % 2, NA_K:, :], preferred_element_type=f32))

    col_max, denom, out = {}, {}, {}
    for k in range(nb + 2):
        if k < nb:
            col_max[k] = scores(k)
        if 1 <= k <= nb:
            denom[k - 1] = probs(k - 1, col_max[k - 1])
        if k >= 2:
            out[k - 2] = weighted(k - 2) / denom[k - 2]
    for i in range(nb):
        o_t = jnp.concatenate([out[i][:NA_HEAD_DIM, :NA_Q], out[i][NA_HEAD_DIM:, NA_Q:]], axis=0)
        o_ref[i * NA_Q:(i + 1) * NA_Q, :] = o_t.T.astype(bf)


def _na_attention(qk, vt, k_meta, vt_meta, tbl, batch, seq):
    n_blocks = seq // NA_Q
    n_steps = n_blocks // NA_BLOCKS_PER_STEP
    n_pairs = NA_HEADS // 2
    kcol = NA_W // LANES
    tq = NA_BLOCKS_PER_STEP * NA_Q
    return pl.pallas_call(
        functools.partial(_na_kernel, n_blocks=n_blocks),
        grid=(n_pairs, batch, n_steps),
        in_specs=[
            pl.BlockSpec((tq, LANES), lambda hp, b, j: (b * n_steps + j, hp)),
            pl.BlockSpec((seq, LANES), lambda hp, b, j: (b, kcol + hp)),
            pl.BlockSpec((LANES, seq), lambda hp, b, j: (hp, b)),
            pl.BlockSpec((N_META, LANES), lambda hp, b, j: (0, kcol + hp)),
            pl.BlockSpec((LANES, N_META), lambda hp, b, j: (hp, 0)),
            pl.BlockSpec((3, 1, NA_KT, 2 * NA_Q), lambda hp, b, j: (0, hp, 0, 0)),
        ],
        out_specs=pl.BlockSpec((tq, LANES), lambda hp, b, j: (b * n_steps + j, hp)),
        out_shape=jax.ShapeDtypeStruct((batch * seq, NA_W), jnp.bfloat16),
        scratch_shapes=[pltpu.VMEM((2, NA_KT, 2 * NA_Q), jnp.float32),
                        pltpu.VMEM((2, NA_KT, 2 * NA_Q), jnp.bfloat16)],
        compiler_params=pltpu.CompilerParams(
            dimension_semantics=("parallel", "parallel", "arbitrary"), vmem_limit_bytes=VMEM_LIMIT),
        name="na_attn",
    )(qk, qk, vt, k_meta, vt_meta, tbl)


def _na_bias_kernel(w_ref, o_ref, *, rows, n_blocks):
    kh = min(NA_WIN_H, rows)
    shape = (GRID_W, LANES)
    lane = lax.broadcasted_iota(jnp.int32, shape, 1)
    kc = lax.broadcasted_iota(jnp.int32, shape, 0)
    qc = lane % GRID_W
    cstart = jnp.clip(qc - NA_WIN_W // 2, 0, GRID_W - NA_WIN_W)
    cvalid = (kc >= cstart) & (kc < cstart + NA_WIN_W)
    neg = jnp.full(shape, NEG, jnp.float32)
    halves = {}

    def toeplitz(hh, dr, half):
        if (hh, dr, half) not in halves:
            row = jnp.broadcast_to(w_ref[hh, dr:dr + 1, :], shape)
            halves[hh, dr, half] = pltpu.roll(row, half * GRID_W, 1, stride=1, stride_axis=0)
        return halves[hh, dr, half]

    for variant, j in enumerate((0, 1, n_blocks - 1)):
        r0 = j * NA_ROWS_PER_BLOCK
        ws = min(max(r0 - NA_WIN_H // 2, 0), rows - NA_KEY_ROWS)
        o_ref[variant, 0, NA_K:, :] = jnp.zeros((N_META, 2 * NA_Q), jnp.float32)
        for hh in range(2):
            for rq_pair in range(NA_ROWS_PER_BLOCK // 2):
                for rk in range(NA_KEY_ROWS):
                    krow = ws + rk
                    tiles = []
                    for half in range(2):
                        r = r0 + 2 * rq_pair + half
                        rs = min(max(r - kh // 2, 0), rows - kh)
                        ok = rs <= krow < rs + kh
                        tiles.append(toeplitz(hh, krow - r + NA_WIN_H - 1, half) if ok else neg)
                    tile = jnp.where(cvalid, jnp.where(lane < GRID_W, tiles[0], tiles[1]), NEG)
                    lane0 = hh * NA_Q + rq_pair * LANES
                    o_ref[variant, 0, rk * GRID_W:(rk + 1) * GRID_W, lane0:lane0 + LANES] = tile


def _na_bias_table(rpb, seq):
    rows = seq // GRID_W
    n_blocks = rows // NA_ROWS_PER_BLOCK
    n_dr = 2 * NA_WIN_H - 1
    n_pairs = NA_HEADS // 2
    rpb = rpb.astype(jnp.float32) * LOG2E
    side = GRID_W - NA_WIN_W
    rpb_pad = jnp.pad(rpb, ((0, 0), (0, 0), (side, side)))
    w = jnp.concatenate([rpb_pad[..., GRID_W - 1:], jnp.zeros(rpb.shape[:2] + (1,), jnp.float32),
                         rpb_pad[..., :GRID_W - 1]], axis=-1)
    w = jnp.roll(jnp.flip(w, axis=-1), 1, axis=-1)
    return pl.pallas_call(
        functools.partial(_na_bias_kernel, rows=rows, n_blocks=n_blocks),
        grid=(n_pairs,),
        in_specs=[pl.BlockSpec((2, n_dr, LANES), lambda hp: (hp, 0, 0))],
        out_specs=pl.BlockSpec((3, 1, NA_KT, 2 * NA_Q), lambda hp: (0, hp, 0, 0)),
        out_shape=jax.ShapeDtypeStruct((3, n_pairs, NA_KT, 2 * NA_Q), jnp.float32),
        compiler_params=pltpu.CompilerParams(dimension_semantics=("parallel",), vmem_limit_bytes=VMEM_LIMIT),
        name="na_bias",
    )(w)


def _mla_kernel(q_ref, k_ref, vt_ref, km_ref, vtm_ref, o_ref, s_ref, p_ref):
    bf = jnp.bfloat16
    f32 = jnp.float32
    n_sub = q_ref.shape[0] // MLA_SUB
    n_chunks = k_ref.shape[0] // MLA_TK
    head = (slice(0, HEAD_GROUP), slice(HEAD_GROUP, 2 * HEAD_GROUP))
    chunk = [slice(c * MLA_TK, (c + 1) * MLA_TK) for c in range(n_chunks)]
    units = [(slice(i * MLA_SUB, (i + 1) * MLA_SUB), hh) for i in range(n_sub) for hh in range(2)]
    n_units = len(units)

    def meta_scores(u):
        rows, hh = units[u]
        return lax.dot_general(km_ref[:, head[hh]], q_ref[rows, head[hh]], _NT, preferred_element_type=f32)

    def scores(u, c):
        rows, hh = units[u]
        s = lax.dot_general(k_ref[chunk[c], head[hh]], q_ref[rows, head[hh]], _NT, preferred_element_type=f32)
        s_ref[u % 2, chunk[c], :] = s
        return _fold_rows(s, jnp.max)

    def probs(u, c, m):
        p_ref[u % 2, chunk[c], :] = jnp.exp2(s_ref[u % 2, chunk[c], :] - m).astype(bf)

    def weighted(u, c):
        hh = units[u][1]
        return jnp.dot(vt_ref[head[hh], chunk[c]], p_ref[u % 2, chunk[c], :], preferred_element_type=f32)

    s_meta, col_max, acc = {}, {}, {}
    for k in range(n_units + 2):
        if k < n_units:
            s_meta[k] = meta_scores(k)
            mx = _fold_rows(s_meta[k], jnp.max)
        if k >= 2:
            p_meta = jnp.exp2(s_meta[k - 2] - col_max[k - 2]).astype(bf)
            acc[k - 2] = jnp.dot(vtm_ref[head[units[k - 2][1]], :], p_meta, preferred_element_type=f32)
        for c in range(n_chunks):
            if k < n_units:
                mx = jnp.maximum(mx, scores(k, c))
            if 1 <= k <= n_units:
                probs(k - 1, c, col_max[k - 1])
            if k >= 2:
                acc[k - 2] = acc[k - 2] + weighted(k - 2, c)
        if k < n_units:
            col_max[k] = jnp.max(mx, axis=0, keepdims=True)

    for i in range(n_sub):
        o0, o1 = [(acc[2 * i + hh] / acc[2 * i + hh][MLA_V_DIM:MLA_V_DIM + 1, :])[:MLA_V_DIM] for hh in range(2)]
        o_ref[units[2 * i][0], :] = jnp.concatenate([o0, o1], axis=0).T.astype(bf)


def _mla_attention(qm, km, vt, km_meta, vt_meta, batch, seq, tq):
    nq = seq // tq
    n_pairs = MLA_HEADS // 2
    pair = 2 * HEAD_GROUP
    return pl.pallas_call(
        _mla_kernel,
        grid=(batch, n_pairs, nq),
        in_specs=[
            pl.BlockSpec((tq, pair), lambda b, hp, i: (b * nq + i, hp)),
            pl.BlockSpec((seq, pair), lambda b, hp, i: (b, hp)),
            pl.BlockSpec((pair, seq), lambda b, hp, i: (hp, b)),
            pl.BlockSpec((N_META, pair), lambda b, hp, i: (0, hp)),
            pl.BlockSpec((pair, N_META), lambda b, hp, i: (hp, 0)),
        ],
        out_specs=pl.BlockSpec((tq, 2 * MLA_V_DIM), lambda b, hp, i: (b * nq + i, hp)),
        out_shape=jax.ShapeDtypeStruct((batch * seq, MLA_HEADS * MLA_V_DIM), jnp.bfloat16),
        scratch_shapes=[pltpu.VMEM((2, seq, MLA_SUB), jnp.float32), pltpu.VMEM((2, seq, MLA_SUB), jnp.bfloat16)],
        compiler_params=pltpu.CompilerParams(
            dimension_semantics=("parallel", "parallel", "arbitrary"), vmem_limit_bytes=VMEM_LIMIT),
        name="mla_attn",
    )(qm, km, vt, km_meta, vt_meta)


def _post_kernel(x_ref, ona_ref, omla_ref, g_ref, wna_ref, wmla_ref, wout_ref, nffn_ref,
                 wff1_ref, wff2_ref, nfin_ref, o_ref):
    bf = jnp.bfloat16
    f32 = jnp.float32
    a = jnp.dot(ona_ref[...], wna_ref[...], preferred_element_type=f32)
    b = jnp.dot(omla_ref[...], wmla_ref[...], preferred_element_type=f32)
    g_na = g_ref[:, :D_MODEL].astype(f32)
    g_mla = g_ref[:, D_MODEL:].astype(f32)
    merged = jax.nn.sigmoid(g_na) * a + jax.nn.sigmoid(g_mla) * b
    h = x_ref[...] + jnp.dot(merged.astype(bf), wout_ref[...], preferred_element_type=f32)
    fn = _rms(h, nffn_ref[...]).astype(bf)
    u = jnp.dot(fn, wff1_ref[...], preferred_element_type=f32)
    u = jnp.square(jnp.maximum(u, 0.0)).astype(bf)
    h = h + jnp.dot(u, wff2_ref[...], preferred_element_type=f32)
    o_ref[...] = _rms(h, nfin_ref[...])


def _post(x2, o_na, o_mla, g, wna, wmla, wout, nffn, wff1, wff2, nfin, tm):
    t = x2.shape[0]
    row = lambda i: (i, 0)
    return pl.pallas_call(
        _post_kernel,
        grid=(t // tm,),
        in_specs=[
            pl.BlockSpec((tm, D_MODEL), row),
            pl.BlockSpec((tm, NA_W), row),
            pl.BlockSpec((tm, MLA_HEADS * MLA_V_DIM), row),
            pl.BlockSpec((tm, 2 * D_MODEL), row),
            _const_spec((NA_W, D_MODEL)),
            _const_spec((MLA_HEADS * MLA_V_DIM, D_MODEL)),
            _const_spec((D_MODEL, D_MODEL)),
            _const_spec((1, D_MODEL)),
            _const_spec((D_MODEL, D_FF)),
            _const_spec((D_FF, D_MODEL)),
            _const_spec((1, D_MODEL)),
        ],
        out_specs=pl.BlockSpec((tm, D_MODEL), row),
        out_shape=jax.ShapeDtypeStruct((t, D_MODEL), jnp.float32),
        compiler_params=pltpu.CompilerParams(
            dimension_semantics=("parallel",), vmem_limit_bytes=VMEM_LIMIT),
        name="post",
    )(x2, o_na, o_mla, g, wna, wmla, wout, nffn, wff1, wff2, nfin)


def _head_groups(w, per_head, lane0):
    k = w.shape[0]
    w = w.reshape(k, MLA_HEADS, per_head)
    w = jnp.pad(w, ((0, 0), (0, 0), (lane0, HEAD_GROUP - lane0 - per_head)))
    return w.reshape(k, MLA_HEADS * HEAD_GROUP)


def _rope_tables(pos, scale):
    inv_freq = 1.0 / (ROPE_THETA ** (jnp.arange(0, MLA_ROPE_DIM, 2, dtype=jnp.float32) / MLA_ROPE_DIM))
    ang = pos[:, None] * inv_freq[None, :]
    cos, sin = jnp.cos(ang), jnp.sin(ang)
    n = pos.shape[0]
    ones = jnp.ones((n, ROPE_LANE0), jnp.float32)
    tail = HEAD_GROUP - ROPE_LANE0 - MLA_ROPE_DIM
    cos_t = jnp.concatenate([ones, cos, cos, jnp.ones((n, tail), jnp.float32)], axis=1) * scale
    sin_t = jnp.concatenate([0 * ones, -sin, sin, jnp.zeros((n, tail), jnp.float32)], axis=1) * scale
    return cos_t, sin_t


def kernel(x, meta, norm_mix, w_in, na_rpb, mla_q_norm, w_uq, mla_kv_norm, w_ukv, w_na_out, w_mla_out,
           w_out, norm_ffn, w_ff1, w_ff2, norm_final):
    assert norm_mix.shape[0] == 1, "single-layer block"
    batch, seq, d = x.shape
    bf = jnp.bfloat16
    f32 = jnp.float32
    x2 = x.reshape(batch * seq, d)

    wi = w_in[0]
    kr_cols = jnp.pad(wi[:, C_KR:C_KR + MLA_ROPE_DIM],
                      ((0, 0), (ROPE_LANE0, HEAD_GROUP - ROPE_LANE0 - MLA_ROPE_DIM)))
    win = jnp.concatenate([wi[:, :C_KR], kr_cols, wi[:, C_KR + MLA_ROPE_DIM:]], axis=1).astype(bf)
    wuq = _head_groups(w_uq[0], MLA_NOPE_DIM + MLA_ROPE_DIM, 0).astype(bf)
    wukv = w_ukv[0].reshape(MLA_KV_RANK, MLA_HEADS, MLA_NOPE_DIM + MLA_V_DIM)
    wuk = _head_groups(wukv[:, :, :MLA_NOPE_DIM].reshape(MLA_KV_RANK, -1), MLA_NOPE_DIM, 0).astype(bf)
    wuv = _head_groups(wukv[:, :, MLA_NOPE_DIM:].reshape(MLA_KV_RANK, -1), MLA_V_DIM, 0).astype(bf)
    vone = (jnp.arange(MLA_QK) % HEAD_GROUP == MLA_V_DIM).astype(f32)[None]

    mla_scale = (MLA_NOPE_DIM + MLA_ROPE_DIM) ** -0.5 * LOG2E
    pos_tok = jnp.arange(N_META, N_META + seq, dtype=f32)
    pos_meta = jnp.arange(META_PAD, dtype=f32)
    cosq, sinq = _rope_tables(pos_tok, mla_scale)
    cosk, sink = _rope_tables(pos_tok, 1.0)
    cosq_m, sinq_m = _rope_tables(pos_meta, mla_scale)
    cosk_m, sink_m = _rope_tables(pos_meta, 1.0)

    small = (norm_mix[0][None], win, mla_q_norm[0][None], wuq, mla_kv_norm[0][None], wuk, wuv)
    qk, vnat, qm, km, vt, g = _inproj(x2, *small, cosq, sinq, cosk, sink, vone, tm=512)
    meta_pad = jnp.pad(meta.astype(f32), ((0, META_PAD - N_META), (0, 0)))
    qk_meta, vnat_meta, _, km_meta, vt_meta, _ = _inproj(meta_pad, *small, cosq_m, sinq_m, cosk_m, sink_m, vone,
                                                         tm=META_PAD)

    tbl = _na_bias_table(na_rpb[0], seq)
    o_na = _na_attention(qk, vnat, qk_meta[:N_META], vnat_meta[:, :N_META], tbl, batch, seq)
    o_mla = _mla_attention(qm, km, vt, km_meta[:N_META], vt_meta[:, :N_META], batch, seq, tq=2048)

    out = _post(x2, o_na, o_mla, g, w_na_out[0].astype(bf), w_mla_out[0].astype(bf), w_out[0].astype(bf),
                norm_ffn[0][None], w_ff1[0].astype(bf), w_ff2[0].astype(bf), norm_final[None], tm=512)
    return out.reshape(batch, seq, d)
```

```python
import functools

import numpy as np
import jax
import jax.numpy as jnp
from jax import lax
from jax.experimental import pallas as pl
from jax.experimental.pallas import tpu as pltpu

D_MODEL = 1024
N_META = 16
GRID_W = 64
NA_HEADS = 8
NA_HEAD_DIM = 64
NA_WIN_H = 8
NA_WIN_W = 16
NA_W = NA_HEADS * NA_HEAD_DIM
MLA_HEADS = 8
MLA_NOPE_DIM = 64
MLA_ROPE_DIM = 32
MLA_V_DIM = 64
MLA_Q_RANK = 384
MLA_KV_RANK = 256
ROPE_THETA = 10000.0
D_FF = 4 * D_MODEL
EPS = 1e-6

LANES = 128
SUBLANES = 8
HEAD_GROUP = 128
ROPE_LANE0 = MLA_NOPE_DIM
MLA_QK = MLA_HEADS * HEAD_GROUP
NEG = -1e30

C_QKV = 0
C_CQ = 3 * NA_W
C_CKV = C_CQ + MLA_Q_RANK
C_KR = C_CKV + MLA_KV_RANK
C_G = C_KR + HEAD_GROUP
D_IN_EXT = C_G + 2 * D_MODEL

NA_ROWS_PER_BLOCK = 4
NA_KEY_ROWS = NA_ROWS_PER_BLOCK + NA_WIN_H
NA_Q = NA_ROWS_PER_BLOCK * GRID_W
NA_K = NA_KEY_ROWS * GRID_W
NA_KT = NA_K + N_META
NA_BLOCKS_PER_STEP = 16
META_PAD = LANES
LOG2E = float(np.log2(np.e))
MLA_TK = 512
MLA_SUB = 512

VMEM_LIMIT = 56 * 1024 * 1024

_NT = (((1,), (1,)), ((), ()))


def _const_spec(shape):
    nd = len(shape)
    return pl.BlockSpec(shape, lambda *_: (0,) * nd, pipeline_mode=pl.Buffered(1))


def _rms(x, g):
    return x * lax.rsqrt(jnp.mean(x * x, axis=-1, keepdims=True) + EPS) * g


def _rope_group(x, cos, sin):
    lane = lax.broadcasted_iota(jnp.int32, x.shape, 1)
    partner = jnp.where(lane < ROPE_LANE0 + MLA_ROPE_DIM // 2,
                        pltpu.roll(x, HEAD_GROUP - MLA_ROPE_DIM // 2, 1),
                        pltpu.roll(x, MLA_ROPE_DIM // 2, 1))
    return x * cos + partner * sin


def _fold_rows(x, op):
    return op(x.reshape(x.shape[0] // SUBLANES, SUBLANES, x.shape[1]), axis=0)


def _inproj_kernel(x_ref, nmix_ref, win_ref, qn_ref, wuq_ref, kvn_ref, wuk_ref, wuv_ref,
                   cosq_ref, sinq_ref, cosk_ref, sink_ref, vone_ref,
                   qk_ref, vnat_ref, qm_ref, km_ref, vt_ref, g_ref):
    bf = jnp.bfloat16
    f32 = jnp.float32
    hn = _rms(x_ref[...], nmix_ref[...]).astype(bf)

    qkv = jnp.dot(hn, win_ref[:, C_QKV:C_CQ], preferred_element_type=f32)
    qk_ref[:, :NA_W] = (qkv[:, :NA_W] * (NA_HEAD_DIM ** -0.5 * LOG2E)).astype(bf)
    qk_ref[:, NA_W:] = qkv[:, NA_W:2 * NA_W].astype(bf)
    vnat_ref[...] = qkv[:, 2 * NA_W:].T.astype(bf)

    g_ref[...] = jnp.dot(hn, win_ref[:, C_G:], preferred_element_type=f32).astype(bf)

    lat = jnp.dot(hn, win_ref[:, C_CQ:C_G], preferred_element_type=f32)
    cq = _rms(lat[:, :MLA_Q_RANK], qn_ref[...]).astype(bf)
    ckv = _rms(lat[:, MLA_Q_RANK:MLA_Q_RANK + MLA_KV_RANK], kvn_ref[...]).astype(bf)
    kr = lat[:, MLA_Q_RANK + MLA_KV_RANK:]
    kr = _rope_group(kr, cosk_ref[...], sink_ref[...])

    q = jnp.dot(cq, wuq_ref[...], preferred_element_type=f32)
    kn = jnp.dot(ckv, wuk_ref[...], preferred_element_type=f32)
    vv = jnp.dot(ckv, wuv_ref[...], preferred_element_type=f32)
    cosq = cosq_ref[...]
    sinq = sinq_ref[...]
    for h in range(MLA_HEADS):
        sl = slice(h * HEAD_GROUP, (h + 1) * HEAD_GROUP)
        qm_ref[:, sl] = _rope_group(q[:, sl], cosq, sinq).astype(bf)
        km_ref[:, sl] = (kn[:, sl] + kr).astype(bf)
    vt_ref[...] = (vv + vone_ref[...]).T.astype(bf)


def _inproj(x2, nmix, win, qn, wuq, kvn, wuk, wuv, cosq, sinq, cosk, sink, vone, tm):
    t = x2.shape[0]
    n_tab = cosq.shape[0] // tm
    row = lambda i: (i, 0)
    col = lambda i: (0, i)
    tab = lambda i: (i % n_tab, 0)
    bf = jnp.bfloat16
    return pl.pallas_call(
        _inproj_kernel,
        grid=(t // tm,),
        in_specs=[
            pl.BlockSpec((tm, D_MODEL), row),
            _const_spec((1, D_MODEL)),
            _const_spec((D_MODEL, D_IN_EXT)),
            _const_spec((1, MLA_Q_RANK)),
            _const_spec((MLA_Q_RANK, MLA_QK)),
            _const_spec((1, MLA_KV_RANK)),
            _const_spec((MLA_KV_RANK, MLA_QK)),
            _const_spec((MLA_KV_RANK, MLA_QK)),
            pl.BlockSpec((tm, HEAD_GROUP), tab),
            pl.BlockSpec((tm, HEAD_GROUP), tab),
            pl.BlockSpec((tm, HEAD_GROUP), tab),
            pl.BlockSpec((tm, HEAD_GROUP), tab),
            _const_spec((1, MLA_QK)),
        ],
        out_specs=[
            pl.BlockSpec((tm, 2 * NA_W), row),
            pl.BlockSpec((NA_W, tm), col),
            pl.BlockSpec((tm, MLA_QK), row),
            pl.BlockSpec((tm, MLA_QK), row),
            pl.BlockSpec((MLA_QK, tm), col),
            pl.BlockSpec((tm, 2 * D_MODEL), row),
        ],
        out_shape=[
            jax.ShapeDtypeStruct((t, 2 * NA_W), bf),
            jax.ShapeDtypeStruct((NA_W, t), bf),
            jax.ShapeDtypeStruct((t, MLA_QK), bf),
            jax.ShapeDtypeStruct((t, MLA_QK), bf),
            jax.ShapeDtypeStruct((MLA_QK, t), bf),
            jax.ShapeDtypeStruct((t, 2 * D_MODEL), bf),
        ],
        compiler_params=pltpu.CompilerParams(
            dimension_semantics=("parallel",), vmem_limit_bytes=VMEM_LIMIT),
        name="inproj",
    )(x2, nmix, win, qn, wuq, kvn, wuk, wuv, cosq, sinq, cosk, sink, vone)


def _na_kernel(q_ref, k_ref, vt_ref, km_ref, vtm_ref, tbl_ref, o_ref, *, n_blocks):
    bf = jnp.bfloat16
    f32 = jnp.float32
    nb = q_ref.shape[0] // NA_Q
    rows = n_blocks * NA_ROWS_PER_BLOCK
    lane = lax.broadcasted_iota(jnp.int32, (NA_Q, LANES), 1)
    in_head = (lane < NA_HEAD_DIM, lane >= NA_HEAD_DIM)

    def window(i):
        j = pl.program_id(2) * nb + i
        ws = jnp.clip(j * NA_ROWS_PER_BLOCK - NA_WIN_H // 2, 0, rows - NA_KEY_ROWS)
        variant = jnp.where(j == 0, 0, jnp.where(j == n_blocks - 1, 2, 1))
        return pl.multiple_of(ws * GRID_W, NA_ROWS_PER_BLOCK * GRID_W), variant

    wins = [window(i) for i in range(nb)]
    s_val, p_val = {}, {}

    def scores(u):
        start, variant = wins[u]
        q2 = q_ref[u * NA_Q:(u + 1) * NA_Q, :]
        zero = jnp.zeros_like(q2)
        qq = jnp.concatenate([jnp.where(in_head[0], q2, zero), jnp.where(in_head[1], q2, zero)], axis=0)
        s_g = lax.dot_general(k_ref[pl.ds(start, NA_K), :], qq, _NT, preferred_element_type=f32)
        s_g = s_g + tbl_ref[variant, 0, :NA_K, :]
        s_m = lax.dot_general(km_ref[...], qq, _NT, preferred_element_type=f32)
        s_m = s_m + tbl_ref[variant, 0, NA_K:, :]
        s_val[u] = (s_g, s_m)
        mx = jnp.maximum(_fold_rows(s_g, jnp.max), _fold_rows(s_m, jnp.max))
        return jnp.max(mx, axis=0, keepdims=True)

    def probs(u, m):
        s_g, s_m = s_val.pop(u)
        p_g = jnp.exp2(s_g - m)
        p_m = jnp.exp2(s_m - m)
        p_val[u] = (p_g.astype(bf), p_m.astype(bf))
        return jnp.sum(_fold_rows(p_g, jnp.sum) + _fold_rows(p_m, jnp.sum), axis=0, keepdims=True)

    def weighted(u):
        start = wins[u][0]
        p_g, p_m = p_val.pop(u)
        return (jnp.dot(vt_ref[:, pl.ds(start, NA_K)], p_g, preferred_element_type=f32)
                + jnp.dot(vtm_ref[...], p_m, preferred_element_type=f32))

    col_max, denom, out = {}, {}, {}
    for k in range(nb + 2):
        if k < nb:
            col_max[k] = scores(k)
        if 1 <= k <= nb:
            denom[k - 1] = probs(k - 1, col_max[k - 1])
        if k >= 2:
            out[k - 2] = weighted(k - 2) / denom[k - 2]
    for i in range(nb):
        o_t = jnp.concatenate([out[i][:NA_HEAD_DIM, :NA_Q], out[i][NA_HEAD_DIM:, NA_Q:]], axis=0)
        o_ref[i * NA_Q:(i + 1) * NA_Q, :] = o_t.T.astype(bf)


def _na_attention(qk, vt, k_meta, vt_meta, tbl, batch, seq):
    n_blocks = seq // NA_Q
    n_steps = n_blocks // NA_BLOCKS_PER_STEP
    n_pairs = NA_HEADS // 2
    kcol = NA_W // LANES
    tq = NA_BLOCKS_PER_STEP * NA_Q
    return pl.pallas_call(
        functools.partial(_na_kernel, n_blocks=n_blocks),
        grid=(n_pairs, batch, n_steps),
        in_specs=[
            pl.BlockSpec((tq, LANES), lambda hp, b, j: (b * n_steps + j, hp)),
            pl.BlockSpec((seq, LANES), lambda hp, b, j: (b, kcol + hp)),
            pl.BlockSpec((LANES, seq), lambda hp, b, j: (hp, b)),
            pl.BlockSpec((N_META, LANES), lambda hp, b, j: (0, kcol + hp)),
            pl.BlockSpec((LANES, N_META), lambda hp, b, j: (hp, 0)),
            pl.BlockSpec((3, 1, NA_KT, 2 * NA_Q), lambda hp, b, j: (0, hp, 0, 0)),
        ],
        out_specs=pl.BlockSpec((tq, LANES), lambda hp, b, j: (b * n_steps + j, hp)),
        out_shape=jax.ShapeDtypeStruct((batch * seq, NA_W), jnp.bfloat16),
        compiler_params=pltpu.CompilerParams(
            dimension_semantics=("parallel", "parallel", "arbitrary"), vmem_limit_bytes=VMEM_LIMIT),
        name="na_attn",
    )(qk, qk, vt, k_meta, vt_meta, tbl)


def _na_bias_kernel(w_ref, o_ref, *, rows, n_blocks):
    kh = min(NA_WIN_H, rows)
    shape = (GRID_W, LANES)
    lane = lax.broadcasted_iota(jnp.int32, shape, 1)
    kc = lax.broadcasted_iota(jnp.int32, shape, 0)
    qc = lane % GRID_W
    cstart = jnp.clip(qc - NA_WIN_W // 2, 0, GRID_W - NA_WIN_W)
    cvalid = (kc >= cstart) & (kc < cstart + NA_WIN_W)
    neg = jnp.full(shape, NEG, jnp.float32)
    halves = {}

    def toeplitz(hh, dr, half):
        if (hh, dr, half) not in halves:
            row = jnp.broadcast_to(w_ref[hh, dr:dr + 1, :], shape)
            halves[hh, dr, half] = pltpu.roll(row, half * GRID_W, 1, stride=1, stride_axis=0)
        return halves[hh, dr, half]

    for variant, j in enumerate((0, 1, n_blocks - 1)):
        r0 = j * NA_ROWS_PER_BLOCK
        ws = min(max(r0 - NA_WIN_H // 2, 0), rows - NA_KEY_ROWS)
        o_ref[variant, 0, NA_K:, :] = jnp.zeros((N_META, 2 * NA_Q), jnp.float32)
        for hh in range(2):
            for rq_pair in range(NA_ROWS_PER_BLOCK // 2):
                for rk in range(NA_KEY_ROWS):
                    krow = ws + rk
                    tiles = []
                    for half in range(2):
                        r = r0 + 2 * rq_pair + half
                        rs = min(max(r - kh // 2, 0), rows - kh)
                        ok = rs <= krow < rs + kh
                        tiles.append(toeplitz(hh, krow - r + NA_WIN_H - 1, half) if ok else neg)
                    tile = jnp.where(cvalid, jnp.where(lane < GRID_W, tiles[0], tiles[1]), NEG)
                    lane0 = hh * NA_Q + rq_pair * LANES
                    o_ref[variant, 0, rk * GRID_W:(rk + 1) * GRID_W, lane0:lane0 + LANES] = tile


def _na_bias_table(rpb, seq):
    rows = seq // GRID_W
    n_blocks = rows // NA_ROWS_PER_BLOCK
    n_dr = 2 * NA_WIN_H - 1
    n_pairs = NA_HEADS // 2
    rpb = rpb.astype(jnp.float32) * LOG2E
    side = GRID_W - NA_WIN_W
    rpb_pad = jnp.pad(rpb, ((0, 0), (0, 0), (side, side)))
    w = jnp.concatenate([rpb_pad[..., GRID_W - 1:], jnp.zeros(rpb.shape[:2] + (1,), jnp.float32),
                         rpb_pad[..., :GRID_W - 1]], axis=-1)
    w = jnp.roll(jnp.flip(w, axis=-1), 1, axis=-1)
    return pl.pallas_call(
        functools.partial(_na_bias_kernel, rows=rows, n_blocks=n_blocks),
        grid=(n_pairs,),
        in_specs=[pl.BlockSpec((2, n_dr, LANES), lambda hp: (hp, 0, 0))],
        out_specs=pl.BlockSpec((3, 1, NA_KT, 2 * NA_Q), lambda hp: (0, hp, 0, 0)),
        out_shape=jax.ShapeDtypeStruct((3, n_pairs, NA_KT, 2 * NA_Q), jnp.float32),
        compiler_params=pltpu.CompilerParams(dimension_semantics=("parallel",), vmem_limit_bytes=VMEM_LIMIT),
        name="na_bias",
    )(w)


def _mla_kernel(q_ref, k_ref, vt_ref, km_ref, vtm_ref, o_ref):
    bf = jnp.bfloat16
    f32 = jnp.float32
    n_sub = q_ref.shape[0] // MLA_SUB
    n_chunks = k_ref.shape[0] // MLA_TK
    head = (slice(0, HEAD_GROUP), slice(HEAD_GROUP, 2 * HEAD_GROUP))
    chunk = [slice(c * MLA_TK, (c + 1) * MLA_TK) for c in range(n_chunks)]
    units = [(slice(i * MLA_SUB, (i + 1) * MLA_SUB), hh) for i in range(n_sub) for hh in range(2)]
    n_units = len(units)

    def meta_scores(u):
        rows, hh = units[u]
        return lax.dot_general(km_ref[:, head[hh]], q_ref[rows, head[hh]], _NT, preferred_element_type=f32)

    s_val, p_val = {}, {}

    def scores(u, c):
        rows, hh = units[u]
        s_val[u, c] = lax.dot_general(k_ref[chunk[c], head[hh]], q_ref[rows, head[hh]], _NT,
                                      preferred_element_type=f32)
        return _fold_rows(s_val[u, c], jnp.max)

    def probs(u, c, m):
        p_val[u, c] = jnp.exp2(s_val.pop((u, c)) - m).astype(bf)

    def weighted(u, c):
        hh = units[u][1]
        return jnp.dot(vt_ref[head[hh], chunk[c]], p_val.pop((u, c)), preferred_element_type=f32)

    s_meta, col_max, acc = {}, {}, {}
    for k in range(n_units + 2):
        if k < n_units:
            s_meta[k] = meta_scores(k)
            mx = _fold_rows(s_meta[k], jnp.max)
        if k >= 2:
            p_meta = jnp.exp2(s_meta[k - 2] - col_max[k - 2]).astype(bf)
            acc[k - 2] = jnp.dot(vtm_ref[head[units[k - 2][1]], :], p_meta, preferred_element_type=f32)
        for c in range(n_chunks):
            if k < n_units:
                mx = jnp.maximum(mx, scores(k, c))
            if 1 <= k <= n_units:
                probs(k - 1, c, col_max[k - 1])
            if k >= 2:
                acc[k - 2] = acc[k - 2] + weighted(k - 2, c)
        if k < n_units:
            col_max[k] = jnp.max(mx, axis=0, keepdims=True)

    for i in range(n_sub):
        o0, o1 = [(acc[2 * i + hh] / acc[2 * i + hh][MLA_V_DIM:MLA_V_DIM + 1, :])[:MLA_V_DIM] for hh in range(2)]
        o_ref[units[2 * i][0], :] = jnp.concatenate([o0, o1], axis=0).T.astype(bf)


def _mla_attention(qm, km, vt, km_meta, vt_meta, batch, seq, tq):
    nq = seq // tq
    n_pairs = MLA_HEADS // 2
    pair = 2 * HEAD_GROUP
    return pl.pallas_call(
        _mla_kernel,
        grid=(batch, n_pairs, nq),
        in_specs=[
            pl.BlockSpec((tq, pair), lambda b, hp, i: (b * nq + i, hp)),
            pl.BlockSpec((seq, pair), lambda b, hp, i: (b, hp)),
            pl.BlockSpec((pair, seq), lambda b, hp, i: (hp, b)),
            pl.BlockSpec((N_META, pair), lambda b, hp, i: (0, hp)),
            pl.BlockSpec((pair, N_META), lambda b, hp, i: (hp, 0)),
        ],
        out_specs=pl.BlockSpec((tq, 2 * MLA_V_DIM), lambda b, hp, i: (b * nq + i, hp)),
        out_shape=jax.ShapeDtypeStruct((batch * seq, MLA_HEADS * MLA_V_DIM), jnp.bfloat16),
        compiler_params=pltpu.CompilerParams(
            dimension_semantics=("parallel", "parallel", "arbitrary"), vmem_limit_bytes=VMEM_LIMIT),
        name="mla_attn",
    )(qm, km, vt, km_meta, vt_meta)


def _post_kernel(x_ref, ona_ref, omla_ref, g_ref, wna_ref, wmla_ref, wout_ref, nffn_ref,
                 wff1_ref, wff2_ref, nfin_ref, o_ref):
    bf = jnp.bfloat16
    f32 = jnp.float32
    a = jnp.dot(ona_ref[...], wna_ref[...], preferred_element_type=f32)
    b = jnp.dot(omla_ref[...], wmla_ref[...], preferred_element_type=f32)
    g_na = g_ref[:, :D_MODEL].astype(f32)
    g_mla = g_ref[:, D_MODEL:].astype(f32)
    merged = jax.nn.sigmoid(g_na) * a + jax.nn.sigmoid(g_mla) * b
    h = x_ref[...] + jnp.dot(merged.astype(bf), wout_ref[...], preferred_element_type=f32)
    fn = _rms(h, nffn_ref[...]).astype(bf)
    u = jnp.dot(fn, wff1_ref[...], preferred_element_type=f32)
    u = jnp.square(jnp.maximum(u, 0.0)).astype(bf)
    h = h + jnp.dot(u, wff2_ref[...], preferred_element_type=f32)
    o_ref[...] = _rms(h, nfin_ref[...])


def _post(x2, o_na, o_mla, g, wna, wmla, wout, nffn, wff1, wff2, nfin, tm):
    t = x2.shape[0]
    row = lambda i: (i, 0)
    return pl.pallas_call(
        _post_kernel,
        grid=(t // tm,),
        in_specs=[
            pl.BlockSpec((tm, D_MODEL), row),
            pl.BlockSpec((tm, NA_W), row),
            pl.BlockSpec((tm, MLA_HEADS * MLA_V_DIM), row),
            pl.BlockSpec((tm, 2 * D_MODEL), row),
            _const_spec((NA_W, D_MODEL)),
            _const_spec((MLA_HEADS * MLA_V_DIM, D_MODEL)),
            _const_spec((D_MODEL, D_MODEL)),
            _const_spec((1, D_MODEL)),
            _const_spec((D_MODEL, D_FF)),
            _const_spec((D_FF, D_MODEL)),
            _const_spec((1, D_MODEL)),
        ],
        out_specs=pl.BlockSpec((tm, D_MODEL), row),
        out_shape=jax.ShapeDtypeStruct((t, D_MODEL), jnp.float32),
        compiler_params=pltpu.CompilerParams(
            dimension_semantics=("parallel",), vmem_limit_bytes=VMEM_LIMIT),
        name="post",
    )(x2, o_na, o_mla, g, wna, wmla, wout, nffn, wff1, wff2, nfin)


def _head_groups(w, per_head, lane0):
    k = w.shape[0]
    w = w.reshape(k, MLA_HEADS, per_head)
    w = jnp.pad(w, ((0, 0), (0, 0), (lane0, HEAD_GROUP - lane0 - per_head)))
    return w.reshape(k, MLA_HEADS * HEAD_GROUP)


def _rope_tables(pos, scale):
    inv_freq = 1.0 / (ROPE_THETA ** (jnp.arange(0, MLA_ROPE_DIM, 2, dtype=jnp.float32) / MLA_ROPE_DIM))
    ang = pos[:, None] * inv_freq[None, :]
    cos, sin = jnp.cos(ang), jnp.sin(ang)
    n = pos.shape[0]
    ones = jnp.ones((n, ROPE_LANE0), jnp.float32)
    tail = HEAD_GROUP - ROPE_LANE0 - MLA_ROPE_DIM
    cos_t = jnp.concatenate([ones, cos, cos, jnp.ones((n, tail), jnp.float32)], axis=1) * scale
    sin_t = jnp.concatenate([0 * ones, -sin, sin, jnp.zeros((n, tail), jnp.float32)], axis=1) * scale
    return cos_t, sin_t


def kernel(x, meta, norm_mix, w_in, na_rpb, mla_q_norm, w_uq, mla_kv_norm, w_ukv, w_na_out, w_mla_out,
           w_out, norm_ffn, w_ff1, w_ff2, norm_final):
    assert norm_mix.shape[0] == 1, "single-layer block"
    batch, seq, d = x.shape
    bf = jnp.bfloat16
    f32 = jnp.float32
    x2 = x.reshape(batch * seq, d)

    wi = w_in[0]
    kr_cols = jnp.pad(wi[:, C_KR:C_KR + MLA_ROPE_DIM],
                      ((0, 0), (ROPE_LANE0, HEAD_GROUP - ROPE_LANE0 - MLA_ROPE_DIM)))
    win = jnp.concatenate([wi[:, :C_KR], kr_cols, wi[:, C_KR + MLA_ROPE_DIM:]], axis=1).astype(bf)
    wuq = _head_groups(w_uq[0], MLA_NOPE_DIM + MLA_ROPE_DIM, 0).astype(bf)
    wukv = w_ukv[0].reshape(MLA_KV_RANK, MLA_HEADS, MLA_NOPE_DIM + MLA_V_DIM)
    wuk = _head_groups(wukv[:, :, :MLA_NOPE_DIM].reshape(MLA_KV_RANK, -1), MLA_NOPE_DIM, 0).astype(bf)
    wuv = _head_groups(wukv[:, :, MLA_NOPE_DIM:].reshape(MLA_KV_RANK, -1), MLA_V_DIM, 0).astype(bf)
    vone = (jnp.arange(MLA_QK) % HEAD_GROUP == MLA_V_DIM).astype(f32)[None]

    mla_scale = (MLA_NOPE_DIM + MLA_ROPE_DIM) ** -0.5 * LOG2E
    pos_tok = jnp.arange(N_META, N_META + seq, dtype=f32)
    pos_meta = jnp.arange(META_PAD, dtype=f32)
    cosq, sinq = _rope_tables(pos_tok, mla_scale)
    cosk, sink = _rope_tables(pos_tok, 1.0)
    cosq_m, sinq_m = _rope_tables(pos_meta, mla_scale)
    cosk_m, sink_m = _rope_tables(pos_meta, 1.0)

    small = (norm_mix[0][None], win, mla_q_norm[0][None], wuq, mla_kv_norm[0][None], wuk, wuv)
    qk, vnat, qm, km, vt, g = _inproj(x2, *small, cosq, sinq, cosk, sink, vone, tm=512)
    meta_pad = jnp.pad(meta.astype(f32), ((0, META_PAD - N_META), (0, 0)))
    qk_meta, vnat_meta, _, km_meta, vt_meta, _ = _inproj(meta_pad, *small, cosq_m, sinq_m, cosk_m, sink_m, vone,
                                                         tm=META_PAD)

    tbl = _na_bias_table(na_rpb[0], seq)
    o_na = _na_attention(qk, vnat, qk_meta[:N_META], vnat_meta[:, :N_META], tbl, batch, seq)
    o_mla = _mla_attention(qm, km, vt, km_meta[:N_META], vt_meta[:, :N_META], batch, seq, tq=2048)

    out = _post(x2, o_na, o_mla, g, w_na_out[0].astype(bf), w_mla_out[0].astype(bf), w_out[0].astype(bf),
                norm_ffn[0][None], w_ff1[0].astype(bf), w_ff2[0].astype(bf), norm_final[None], tm=512)
    return out.reshape(batch, seq, d)
```

```python
import functools

import numpy as np
import jax
import jax.numpy as jnp
from jax import lax
from jax.experimental import pallas as pl
from jax.experimental.pallas import tpu as pltpu

D_MODEL = 1024
N_META = 16
GRID_W = 64
NA_HEADS = 8
NA_HEAD_DIM = 64
NA_WIN_H = 8
NA_WIN_W = 16
NA_W = NA_HEADS * NA_HEAD_DIM
MLA_HEADS = 8
MLA_NOPE_DIM = 64
MLA_ROPE_DIM = 32
MLA_V_DIM = 64
MLA_Q_RANK = 384
MLA_KV_RANK = 256
ROPE_THETA = 10000.0
D_FF = 4 * D_MODEL
EPS = 1e-6

LANES = 128
SUBLANES = 8
HEAD_GROUP = 128
ROPE_LANE0 = MLA_NOPE_DIM
MLA_QK = MLA_HEADS * HEAD_GROUP
NEG = -1e30

C_QKV = 0
C_CQ = 3 * NA_W
C_CKV = C_CQ + MLA_Q_RANK
C_KR = C_CKV + MLA_KV_RANK
C_G = C_KR + HEAD_GROUP
D_IN_EXT = C_G + 2 * D_MODEL

NA_ROWS_PER_BLOCK = 4
NA_KEY_ROWS = NA_ROWS_PER_BLOCK + NA_WIN_H
NA_Q = NA_ROWS_PER_BLOCK * GRID_W
NA_K = NA_KEY_ROWS * GRID_W
NA_KT = NA_K + N_META
NA_BLOCKS_PER_STEP = 16
META_PAD = LANES
LOG2E = float(np.log2(np.e))
MLA_TK = 512
MLA_SUB = 512

VMEM_LIMIT = 56 * 1024 * 1024

_NT = (((1,), (1,)), ((), ()))


def _const_spec(shape):
    nd = len(shape)
    return pl.BlockSpec(shape, lambda *_: (0,) * nd, pipeline_mode=pl.Buffered(1))


def _rms(x, g):
    return x * lax.rsqrt(jnp.mean(x * x, axis=-1, keepdims=True) + EPS) * g


def _rope_group(x, cos, sin):
    lane = lax.broadcasted_iota(jnp.int32, x.shape, 1)
    partner = jnp.where(lane < ROPE_LANE0 + MLA_ROPE_DIM // 2,
                        pltpu.roll(x, HEAD_GROUP - MLA_ROPE_DIM // 2, 1),
                        pltpu.roll(x, MLA_ROPE_DIM // 2, 1))
    return x * cos + partner * sin


def _fold_rows(x, op):
    return op(x.reshape(x.shape[0] // SUBLANES, SUBLANES, x.shape[1]), axis=0)


def _inproj_kernel(x_ref, nmix_ref, win_ref, qn_ref, wuq_ref, kvn_ref, wuk_ref, wuv_ref,
                   cosq_ref, sinq_ref, cosk_ref, sink_ref, vone_ref,
                   qk_ref, vnat_ref, qm_ref, km_ref, vt_ref, g_ref):
    bf = jnp.bfloat16
    f32 = jnp.float32
    hn = _rms(x_ref[...], nmix_ref[...]).astype(bf)

    lat = jnp.dot(hn, win_ref[:, C_CQ:C_G], preferred_element_type=f32)
    cq = _rms(lat[:, :MLA_Q_RANK], qn_ref[...]).astype(bf)
    ckv = _rms(lat[:, MLA_Q_RANK:MLA_Q_RANK + MLA_KV_RANK], kvn_ref[...]).astype(bf)
    kr = lat[:, MLA_Q_RANK + MLA_KV_RANK:]
    kr = _rope_group(kr, cosk_ref[...], sink_ref[...])

    qkv = jnp.dot(hn, win_ref[:, C_QKV:C_CQ], preferred_element_type=f32)
    qk_ref[:, :NA_W] = (qkv[:, :NA_W] * (NA_HEAD_DIM ** -0.5 * LOG2E)).astype(bf)
    qk_ref[:, NA_W:] = qkv[:, NA_W:2 * NA_W].astype(bf)
    vnat_ref[...] = qkv[:, 2 * NA_W:].T.astype(bf)

    q = jnp.dot(cq, wuq_ref[...], preferred_element_type=f32)
    kn = jnp.dot(ckv, wuk_ref[...], preferred_element_type=f32)
    vv = jnp.dot(ckv, wuv_ref[...], preferred_element_type=f32)

    g_ref[...] = jnp.dot(hn, win_ref[:, C_G:], preferred_element_type=f32).astype(bf)

    cosq = cosq_ref[...]
    sinq = sinq_ref[...]
    for h in range(MLA_HEADS):
        sl = slice(h * HEAD_GROUP, (h + 1) * HEAD_GROUP)
        qm_ref[:, sl] = _rope_group(q[:, sl], cosq, sinq).astype(bf)
        km_ref[:, sl] = (kn[:, sl] + kr).astype(bf)
    vt_ref[...] = (vv + vone_ref[...]).T.astype(bf)


def _inproj(x2, nmix, win, qn, wuq, kvn, wuk, wuv, cosq, sinq, cosk, sink, vone, tm):
    t = x2.shape[0]
    n_tab = cosq.shape[0] // tm
    row = lambda i: (i, 0)
    col = lambda i: (0, i)
    tab = lambda i: (i % n_tab, 0)
    bf = jnp.bfloat16
    return pl.pallas_call(
        _inproj_kernel,
        grid=(t // tm,),
        in_specs=[
            pl.BlockSpec((tm, D_MODEL), row),
            _const_spec((1, D_MODEL)),
            _const_spec((D_MODEL, D_IN_EXT)),
            _const_spec((1, MLA_Q_RANK)),
            _const_spec((MLA_Q_RANK, MLA_QK)),
            _const_spec((1, MLA_KV_RANK)),
            _const_spec((MLA_KV_RANK, MLA_QK)),
            _const_spec((MLA_KV_RANK, MLA_QK)),
            pl.BlockSpec((tm, HEAD_GROUP), tab),
            pl.BlockSpec((tm, HEAD_GROUP), tab),
            pl.BlockSpec((tm, HEAD_GROUP), tab),
            pl.BlockSpec((tm, HEAD_GROUP), tab),
            _const_spec((1, MLA_QK)),
        ],
        out_specs=[
            pl.BlockSpec((tm, 2 * NA_W), row),
            pl.BlockSpec((NA_W, tm), col),
            pl.BlockSpec((tm, MLA_QK), row),
            pl.BlockSpec((tm, MLA_QK), row),
            pl.BlockSpec((MLA_QK, tm), col),
            pl.BlockSpec((tm, 2 * D_MODEL), row),
        ],
        out_shape=[
            jax.ShapeDtypeStruct((t, 2 * NA_W), bf),
            jax.ShapeDtypeStruct((NA_W, t), bf),
            jax.ShapeDtypeStruct((t, MLA_QK), bf),
            jax.ShapeDtypeStruct((t, MLA_QK), bf),
            jax.ShapeDtypeStruct((MLA_QK, t), bf),
            jax.ShapeDtypeStruct((t, 2 * D_MODEL), bf),
        ],
        compiler_params=pltpu.CompilerParams(
            dimension_semantics=("parallel",), vmem_limit_bytes=VMEM_LIMIT),
        name="inproj",
    )(x2, nmix, win, qn, wuq, kvn, wuk, wuv, cosq, sinq, cosk, sink, vone)


def _na_kernel(q_ref, k_ref, vt_ref, km_ref, vtm_ref, tbl_ref, o_ref, *, n_blocks):
    bf = jnp.bfloat16
    f32 = jnp.float32
    nb = q_ref.shape[0] // NA_Q
    rows = n_blocks * NA_ROWS_PER_BLOCK
    lane = lax.broadcasted_iota(jnp.int32, (NA_Q, LANES), 1)
    in_head = (lane < NA_HEAD_DIM, lane >= NA_HEAD_DIM)

    def window(i):
        j = pl.program_id(2) * nb + i
        ws = jnp.clip(j * NA_ROWS_PER_BLOCK - NA_WIN_H // 2, 0, rows - NA_KEY_ROWS)
        variant = jnp.where(j == 0, 0, jnp.where(j == n_blocks - 1, 2, 1))
        return pl.multiple_of(ws * GRID_W, NA_ROWS_PER_BLOCK * GRID_W), variant

    wins = [window(i) for i in range(nb)]
    s_val, p_val = {}, {}

    def scores(u):
        start, variant = wins[u]
        q2 = q_ref[u * NA_Q:(u + 1) * NA_Q, :]
        zero = jnp.zeros_like(q2)
        qq = jnp.concatenate([jnp.where(in_head[0], q2, zero), jnp.where(in_head[1], q2, zero)], axis=0)
        s_g = lax.dot_general(k_ref[pl.ds(start, NA_K), :], qq, _NT, preferred_element_type=f32)
        s_g = s_g + tbl_ref[variant, 0, :NA_K, :]
        s_m = lax.dot_general(km_ref[...], qq, _NT, preferred_element_type=f32)
        s_m = s_m + tbl_ref[variant, 0, NA_K:, :]
        s_val[u] = (s_g, s_m)
        mx = jnp.maximum(_fold_rows(s_g, jnp.max), _fold_rows(s_m, jnp.max))
        return jnp.max(mx, axis=0, keepdims=True)

    def probs(u, m):
        s_g, s_m = s_val.pop(u)
        p_g = jnp.exp2(s_g - m)
        p_m = jnp.exp2(s_m - m)
        p_val[u] = (p_g.astype(bf), p_m.astype(bf))
        return jnp.sum(_fold_rows(p_g, jnp.sum) + _fold_rows(p_m, jnp.sum), axis=0, keepdims=True)

    def weighted(u):
        start = wins[u][0]
        p_g, p_m = p_val.pop(u)
        return (jnp.dot(vt_ref[:, pl.ds(start, NA_K)], p_g, preferred_element_type=f32)
                + jnp.dot(vtm_ref[...], p_m, preferred_element_type=f32))

    col_max, denom, out = {}, {}, {}
    for k in range(nb + 2):
        if k < nb:
            col_max[k] = scores(k)
        if 1 <= k <= nb:
            denom[k - 1] = probs(k - 1, col_max[k - 1])
        if k >= 2:
            out[k - 2] = weighted(k - 2) / denom[k - 2]
    for i in range(nb):
        o_t = jnp.concatenate([out[i][:NA_HEAD_DIM, :NA_Q], out[i][NA_HEAD_DIM:, NA_Q:]], axis=0)
        o_ref[i * NA_Q:(i + 1) * NA_Q, :] = o_t.T.astype(bf)


def _na_attention(qk, vt, k_meta, vt_meta, tbl, batch, seq):
    n_blocks = seq // NA_Q
    n_steps = n_blocks // NA_BLOCKS_PER_STEP
    n_pairs = NA_HEADS // 2
    kcol = NA_W // LANES
    tq = NA_BLOCKS_PER_STEP * NA_Q
    return pl.pallas_call(
        functools.partial(_na_kernel, n_blocks=n_blocks),
        grid=(n_pairs, batch, n_steps),
        in_specs=[
            pl.BlockSpec((tq, LANES), lambda hp, b, j: (b * n_steps + j, hp)),
            pl.BlockSpec((seq, LANES), lambda hp, b, j: (b, kcol + hp)),
            pl.BlockSpec((LANES, seq), lambda hp, b, j: (hp, b)),
            pl.BlockSpec((N_META, LANES), lambda hp, b, j: (0, kcol + hp)),
            pl.BlockSpec((LANES, N_META), lambda hp, b, j: (hp, 0)),
            pl.BlockSpec((3, 1, NA_KT, 2 * NA_Q), lambda hp, b, j: (0, hp, 0, 0)),
        ],
        out_specs=pl.BlockSpec((tq, LANES), lambda hp, b, j: (b * n_steps + j, hp)),
        out_shape=jax.ShapeDtypeStruct((batch * seq, NA_W), jnp.bfloat16),
        compiler_params=pltpu.CompilerParams(
            dimension_semantics=("parallel", "parallel", "arbitrary"), vmem_limit_bytes=VMEM_LIMIT),
        name="na_attn",
    )(qk, qk, vt, k_meta, vt_meta, tbl)


def _na_bias_kernel(w_ref, o_ref, *, rows, n_blocks):
    kh = min(NA_WIN_H, rows)
    shape = (GRID_W, LANES)
    lane = lax.broadcasted_iota(jnp.int32, shape, 1)
    kc = lax.broadcasted_iota(jnp.int32, shape, 0)
    qc = lane % GRID_W
    cstart = jnp.clip(qc - NA_WIN_W // 2, 0, GRID_W - NA_WIN_W)
    cvalid = (kc >= cstart) & (kc < cstart + NA_WIN_W)
    neg = jnp.full(shape, NEG, jnp.float32)
    halves = {}

    def toeplitz(hh, dr, half):
        if (hh, dr, half) not in halves:
            row = jnp.broadcast_to(w_ref[hh, dr:dr + 1, :], shape)
            halves[hh, dr, half] = pltpu.roll(row, half * GRID_W, 1, stride=1, stride_axis=0)
        return halves[hh, dr, half]

    for variant, j in enumerate((0, 1, n_blocks - 1)):
        r0 = j * NA_ROWS_PER_BLOCK
        ws = min(max(r0 - NA_WIN_H // 2, 0), rows - NA_KEY_ROWS)
        o_ref[variant, 0, NA_K:, :] = jnp.zeros((N_META, 2 * NA_Q), jnp.float32)
        for hh in range(2):
            for rq_pair in range(NA_ROWS_PER_BLOCK // 2):
                for rk in range(NA_KEY_ROWS):
                    krow = ws + rk
                    tiles = []
                    for half in range(2):
                        r = r0 + 2 * rq_pair + half
                        rs = min(max(r - kh // 2, 0), rows - kh)
                        ok = rs <= krow < rs + kh
                        tiles.append(toeplitz(hh, krow - r + NA_WIN_H - 1, half) if ok else neg)
                    tile = jnp.where(cvalid, jnp.where(lane < GRID_W, tiles[0], tiles[1]), NEG)
                    lane0 = hh * NA_Q + rq_pair * LANES
                    o_ref[variant, 0, rk * GRID_W:(rk + 1) * GRID_W, lane0:lane0 + LANES] = tile


def _na_bias_table(rpb, seq):
    rows = seq // GRID_W
    n_blocks = rows // NA_ROWS_PER_BLOCK
    n_dr = 2 * NA_WIN_H - 1
    n_pairs = NA_HEADS // 2
    rpb = rpb.astype(jnp.float32) * LOG2E
    side = GRID_W - NA_WIN_W
    rpb_pad = jnp.pad(rpb, ((0, 0), (0, 0), (side, side)))
    w = jnp.concatenate([rpb_pad[..., GRID_W - 1:], jnp.zeros(rpb.shape[:2] + (1,), jnp.float32),
                         rpb_pad[..., :GRID_W - 1]], axis=-1)
    w = jnp.roll(jnp.flip(w, axis=-1), 1, axis=-1)
    return pl.pallas_call(
        functools.partial(_na_bias_kernel, rows=rows, n_blocks=n_blocks),
        grid=(n_pairs,),
        in_specs=[pl.BlockSpec((2, n_dr, LANES), lambda hp: (hp, 0, 0))],
        out_specs=pl.BlockSpec((3, 1, NA_KT, 2 * NA_Q), lambda hp: (0, hp, 0, 0)),
        out_shape=jax.ShapeDtypeStruct((3, n_pairs, NA_KT, 2 * NA_Q), jnp.float32),
        compiler_params=pltpu.CompilerParams(dimension_semantics=("parallel",), vmem_limit_bytes=VMEM_LIMIT),
        name="na_bias",
    )(w)


def _mla_kernel(q_ref, k_ref, vt_ref, km_ref, vtm_ref, o_ref, s_ref, p_ref):
    bf = jnp.bfloat16
    f32 = jnp.float32
    n_sub = q_ref.shape[0] // MLA_SUB
    n_chunks = k_ref.shape[0] // MLA_TK
    head = (slice(0, HEAD_GROUP), slice(HEAD_GROUP, 2 * HEAD_GROUP))
    chunk = [slice(c * MLA_TK, (c + 1) * MLA_TK) for c in range(n_chunks)]
    units = [(slice(i * MLA_SUB, (i + 1) * MLA_SUB), hh) for i in range(n_sub) for hh in range(2)]
    n_units = len(units)

    def meta_scores(u):
        rows, hh = units[u]
        return lax.dot_general(km_ref[:, head[hh]], q_ref[rows, head[hh]], _NT, preferred_element_type=f32)

    def scores(u, c):
        rows, hh = units[u]
        s = lax.dot_general(k_ref[chunk[c], head[hh]], q_ref[rows, head[hh]], _NT, preferred_element_type=f32)
        s_ref[u % 2, chunk[c], :] = s
        return _fold_rows(s, jnp.max)

    def probs(u, c, m):
        p_ref[u % 2, chunk[c], :] = jnp.exp2(s_ref[u % 2, chunk[c], :] - m).astype(bf)

    def weighted(u, c):
        hh = units[u][1]
        return jnp.dot(vt_ref[head[hh], chunk[c]], p_ref[u % 2, chunk[c], :], preferred_element_type=f32)

    s_meta, col_max, acc = {}, {}, {}
    for k in range(n_units + 2):
        if k < n_units:
            s_meta[k] = meta_scores(k)
            mx = _fold_rows(s_meta[k], jnp.max)
        if k >= 2:
            p_meta = jnp.exp2(s_meta[k - 2] - col_max[k - 2]).astype(bf)
            acc[k - 2] = jnp.dot(vtm_ref[head[units[k - 2][1]], :], p_meta, preferred_element_type=f32)
        for c in range(n_chunks):
            if k < n_units:
                mx = jnp.maximum(mx, scores(k, c))
            if 1 <= k <= n_units:
                probs(k - 1, c, col_max[k - 1])
            if k >= 2:
                acc[k - 2] = acc[k - 2] + weighted(k - 2, c)
        if k < n_units:
            col_max[k] = jnp.max(mx, axis=0, keepdims=True)

    for i in range(n_sub):
        o0, o1 = [(acc[2 * i + hh] / acc[2 * i + hh][MLA_V_DIM:MLA_V_DIM + 1, :])[:MLA_V_DIM] for hh in range(2)]
        o_ref[units[2 * i][0], :] = jnp.concatenate([o0, o1], axis=0).T.astype(bf)


def _mla_attention(qm, km, vt, km_meta, vt_meta, batch, seq, tq):
    nq = seq // tq
    n_pairs = MLA_HEADS // 2
    pair = 2 * HEAD_GROUP
    return pl.pallas_call(
        _mla_kernel,
        grid=(batch, n_pairs, nq),
        in_specs=[
            pl.BlockSpec((tq, pair), lambda b, hp, i: (b * nq + i, hp)),
            pl.BlockSpec((seq, pair), lambda b, hp, i: (b, hp)),
            pl.BlockSpec((pair, seq), lambda b, hp, i: (hp, b)),
            pl.BlockSpec((N_META, pair), lambda b, hp, i: (0, hp)),
            pl.BlockSpec((pair, N_META), lambda b, hp, i: (hp, 0)),
        ],
        out_specs=pl.BlockSpec((tq, 2 * MLA_V_DIM), lambda b, hp, i: (b * nq + i, hp)),
        out_shape=jax.ShapeDtypeStruct((batch * seq, MLA_HEADS * MLA_V_DIM), jnp.bfloat16),
        scratch_shapes=[pltpu.VMEM((2, seq, MLA_SUB), jnp.float32), pltpu.VMEM((2, seq, MLA_SUB), jnp.bfloat16)],
        compiler_params=pltpu.CompilerParams(
            dimension_semantics=("parallel", "parallel", "arbitrary"), vmem_limit_bytes=VMEM_LIMIT),
        name="mla_attn",
    )(qm, km, vt, km_meta, vt_meta)


def _post_kernel(x_ref, ona_ref, omla_ref, g_ref, wna_ref, wmla_ref, wout_ref, nffn_ref,
                 wff1_ref, wff2_ref, nfin_ref, o_ref):
    bf = jnp.bfloat16
    f32 = jnp.float32
    a = jnp.dot(ona_ref[...], wna_ref[...], preferred_element_type=f32)
    b = jnp.dot(omla_ref[...], wmla_ref[...], preferred_element_type=f32)
    g_na = g_ref[:, :D_MODEL].astype(f32)
    g_mla = g_ref[:, D_MODEL:].astype(f32)
    merged = jax.nn.sigmoid(g_na) * a + jax.nn.sigmoid(g_mla) * b
    h = x_ref[...] + jnp.dot(merged.astype(bf), wout_ref[...], preferred_element_type=f32)
    fn = _rms(h, nffn_ref[...]).astype(bf)
    u = jnp.dot(fn, wff1_ref[...], preferred_element_type=f32)
    u = jnp.square(jnp.maximum(u, 0.0)).astype(bf)
    h = h + jnp.dot(u, wff2_ref[...], preferred_element_type=f32)
    o_ref[...] = _rms(h, nfin_ref[...])


def _post(x2, o_na, o_mla, g, wna, wmla, wout, nffn, wff1, wff2, nfin, tm):
    t = x2.shape[0]
    row = lambda i: (i, 0)
    return pl.pallas_call(
        _post_kernel,
        grid=(t // tm,),
        in_specs=[
            pl.BlockSpec((tm, D_MODEL), row),
            pl.BlockSpec((tm, NA_W), row),
            pl.BlockSpec((tm, MLA_HEADS * MLA_V_DIM), row),
            pl.BlockSpec((tm, 2 * D_MODEL), row),
            _const_spec((NA_W, D_MODEL)),
            _const_spec((MLA_HEADS * MLA_V_DIM, D_MODEL)),
            _const_spec((D_MODEL, D_MODEL)),
            _const_spec((1, D_MODEL)),
            _const_spec((D_MODEL, D_FF)),
            _const_spec((D_FF, D_MODEL)),
            _const_spec((1, D_MODEL)),
        ],
        out_specs=pl.BlockSpec((tm, D_MODEL), row),
        out_shape=jax.ShapeDtypeStruct((t, D_MODEL), jnp.float32),
        compiler_params=pltpu.CompilerParams(
            dimension_semantics=("parallel",), vmem_limit_bytes=VMEM_LIMIT),
        name="post",
    )(x2, o_na, o_mla, g, wna, wmla, wout, nffn, wff1, wff2, nfin)


def _head_groups(w, per_head, lane0):
    k = w.shape[0]
    w = w.reshape(k, MLA_HEADS, per_head)
    w = jnp.pad(w, ((0, 0), (0, 0), (lane0, HEAD_GROUP - lane0 - per_head)))
    return w.reshape(k, MLA_HEADS * HEAD_GROUP)


def _rope_tables(pos, scale):
    inv_freq = 1.0 / (ROPE_THETA ** (jnp.arange(0, MLA_ROPE_DIM, 2, dtype=jnp.float32) / MLA_ROPE_DIM))
    ang = pos[:, None] * inv_freq[None, :]
    cos, sin = jnp.cos(ang), jnp.sin(ang)
    n = pos.shape[0]
    ones = jnp.ones((n, ROPE_LANE0), jnp.float32)
    tail = HEAD_GROUP - ROPE_LANE0 - MLA_ROPE_DIM
    cos_t = jnp.concatenate([ones, cos, cos, jnp.ones((n, tail), jnp.float32)], axis=1) * scale
    sin_t = jnp.concatenate([0 * ones, -sin, sin, jnp.zeros((n, tail), jnp.float32)], axis=1) * scale
    return cos_t, sin_t


def kernel(x, meta, norm_mix, w_in, na_rpb, mla_q_norm, w_uq, mla_kv_norm, w_ukv, w_na_out, w_mla_out,
           w_out, norm_ffn, w_ff1, w_ff2, norm_final):
    assert norm_mix.shape[0] == 1, "single-layer block"
    batch, seq, d = x.shape
    bf = jnp.bfloat16
    f32 = jnp.float32
    x2 = x.reshape(batch * seq, d)

    wi = w_in[0]
    kr_cols = jnp.pad(wi[:, C_KR:C_KR + MLA_ROPE_DIM],
                      ((0, 0), (ROPE_LANE0, HEAD_GROUP - ROPE_LANE0 - MLA_ROPE_DIM)))
    win = jnp.concatenate([wi[:, :C_KR], kr_cols, wi[:, C_KR + MLA_ROPE_DIM:]], axis=1).astype(bf)
    wuq = _head_groups(w_uq[0], MLA_NOPE_DIM + MLA_ROPE_DIM, 0).astype(bf)
    wukv = w_ukv[0].reshape(MLA_KV_RANK, MLA_HEADS, MLA_NOPE_DIM + MLA_V_DIM)
    wuk = _head_groups(wukv[:, :, :MLA_NOPE_DIM].reshape(MLA_KV_RANK, -1), MLA_NOPE_DIM, 0).astype(bf)
    wuv = _head_groups(wukv[:, :, MLA_NOPE_DIM:].reshape(MLA_KV_RANK, -1), MLA_V_DIM, 0).astype(bf)
    vone = (jnp.arange(MLA_QK) % HEAD_GROUP == MLA_V_DIM).astype(f32)[None]

    mla_scale = (MLA_NOPE_DIM + MLA_ROPE_DIM) ** -0.5 * LOG2E
    pos_tok = jnp.arange(N_META, N_META + seq, dtype=f32)
    pos_meta = jnp.arange(META_PAD, dtype=f32)
    cosq, sinq = _rope_tables(pos_tok, mla_scale)
    cosk, sink = _rope_tables(pos_tok, 1.0)
    cosq_m, sinq_m = _rope_tables(pos_meta, mla_scale)
    cosk_m, sink_m = _rope_tables(pos_meta, 1.0)

    small = (norm_mix[0][None], win, mla_q_norm[0][None], wuq, mla_kv_norm[0][None], wuk, wuv)
    qk, vnat, qm, km, vt, g = _inproj(x2, *small, cosq, sinq, cosk, sink, vone, tm=512)
    meta_pad = jnp.pad(meta.astype(f32), ((0, META_PAD - N_META), (0, 0)))
    qk_meta, vnat_meta, _, km_meta, vt_meta, _ = _inproj(meta_pad, *small, cosq_m, sinq_m, cosk_m, sink_m, vone,
                                                         tm=META_PAD)

    tbl = _na_bias_table(na_rpb[0], seq)
    o_na = _na_attention(qk, vnat, qk_meta[:N_META], vnat_meta[:, :N_META], tbl, batch, seq)
    o_mla = _mla_attention(qm, km, vt, km_meta[:N_META], vt_meta[:, :N_META], batch, seq, tq=2048)

    out = _post(x2, o_na, o_mla, g, w_na_out[0].astype(bf), w_mla_out[0].astype(bf), w_out[0].astype(bf),
                norm_ffn[0][None], w_ff1[0].astype(bf), w_ff2[0].astype(bf), norm_final[None], tm=512)
    return out.reshape(batch, seq, d)
```

```python
import functools

import numpy as np
import jax
import jax.numpy as jnp
from jax import lax
from jax.experimental import pallas as pl
from jax.experimental.pallas import tpu as pltpu

D_MODEL = 1024
N_META = 16
GRID_W = 64
NA_HEADS = 8
NA_HEAD_DIM = 64
NA_WIN_H = 8
NA_WIN_W = 16
NA_W = NA_HEADS * NA_HEAD_DIM
MLA_HEADS = 8
MLA_NOPE_DIM = 64
MLA_ROPE_DIM = 32
MLA_V_DIM = 64
MLA_Q_RANK = 384
MLA_KV_RANK = 256
ROPE_THETA = 10000.0
D_FF = 4 * D_MODEL
EPS = 1e-6

LANES = 128
SUBLANES = 8
HEAD_GROUP = 128
ROPE_LANE0 = MLA_NOPE_DIM
MLA_QK = MLA_HEADS * HEAD_GROUP
NEG = -1e30

C_QKV = 0
C_CQ = 3 * NA_W
C_CKV = C_CQ + MLA_Q_RANK
C_KR = C_CKV + MLA_KV_RANK
C_G = C_KR + HEAD_GROUP
D_IN_EXT = C_G + 2 * D_MODEL

NA_ROWS_PER_BLOCK = 4
NA_KEY_ROWS = NA_ROWS_PER_BLOCK + NA_WIN_H
NA_Q = NA_ROWS_PER_BLOCK * GRID_W
NA_K = NA_KEY_ROWS * GRID_W
NA_KT = NA_K + N_META
NA_VT_ROWS = LANES + 16
NA_VT_ALL = (NA_HEADS // 2) * NA_VT_ROWS
NA_BLOCKS_PER_STEP = 16
META_PAD = LANES
LOG2E = float(np.log2(np.e))
MLA_TK = 512
MLA_SUB = 512

VMEM_LIMIT = 56 * 1024 * 1024

_NT = (((1,), (1,)), ((), ()))


def _const_spec(shape):
    nd = len(shape)
    return pl.BlockSpec(shape, lambda *_: (0,) * nd, pipeline_mode=pl.Buffered(1))


def _rms(x, g):
    return x * lax.rsqrt(jnp.mean(x * x, axis=-1, keepdims=True) + EPS) * g


def _rope_group(x, cos, sin):
    lane = lax.broadcasted_iota(jnp.int32, x.shape, 1)
    partner = jnp.where(lane < ROPE_LANE0 + MLA_ROPE_DIM // 2,
                        pltpu.roll(x, HEAD_GROUP - MLA_ROPE_DIM // 2, 1),
                        pltpu.roll(x, MLA_ROPE_DIM // 2, 1))
    return x * cos + partner * sin


def _fold_rows(x, op):
    return op(x.reshape(x.shape[0] // SUBLANES, SUBLANES, x.shape[1]), axis=0)


def _inproj_kernel(x_ref, nmix_ref, win_ref, qn_ref, wuq_ref, kvn_ref, wuk_ref, wuv_ref,
                   cosq_ref, sinq_ref, cosk_ref, sink_ref, vone_ref,
                   qk_ref, vnat_ref, qm_ref, km_ref, vt_ref, g_ref):
    bf = jnp.bfloat16
    f32 = jnp.float32
    hn = _rms(x_ref[...], nmix_ref[...]).astype(bf)

    lat = jnp.dot(hn, win_ref[:, C_CQ:C_G], preferred_element_type=f32)
    cq = _rms(lat[:, :MLA_Q_RANK], qn_ref[...]).astype(bf)
    ckv = _rms(lat[:, MLA_Q_RANK:MLA_Q_RANK + MLA_KV_RANK], kvn_ref[...]).astype(bf)
    kr = lat[:, MLA_Q_RANK + MLA_KV_RANK:]
    kr = _rope_group(kr, cosk_ref[...], sink_ref[...])

    qkv = jnp.dot(hn, win_ref[:, C_QKV:C_CQ], preferred_element_type=f32)
    qk_ref[:, :NA_W] = (qkv[:, :NA_W] * (NA_HEAD_DIM ** -0.5 * LOG2E)).astype(bf)
    qk_ref[:, NA_W:] = qkv[:, NA_W:2 * NA_W].astype(bf)
    v_t = qkv[:, 2 * NA_W:].T.astype(bf)
    row = lax.broadcasted_iota(jnp.int32, (NA_VT_ROWS - LANES, v_t.shape[1]), 0)
    ones_rows = jnp.where(row == 0, 1.0, 0.0).astype(bf)
    for hp in range(NA_HEADS // 2):
        vnat_ref[hp * NA_VT_ROWS:hp * NA_VT_ROWS + LANES, :] = v_t[hp * LANES:(hp + 1) * LANES]
        vnat_ref[hp * NA_VT_ROWS + LANES:(hp + 1) * NA_VT_ROWS, :] = ones_rows

    q = jnp.dot(cq, wuq_ref[...], preferred_element_type=f32)
    kn = jnp.dot(ckv, wuk_ref[...], preferred_element_type=f32)
    vv = jnp.dot(ckv, wuv_ref[...], preferred_element_type=f32)

    g_ref[...] = jnp.dot(hn, win_ref[:, C_G:], preferred_element_type=f32).astype(bf)

    cosq = cosq_ref[...]
    sinq = sinq_ref[...]
    for h in range(MLA_HEADS):
        sl = slice(h * HEAD_GROUP, (h + 1) * HEAD_GROUP)
        qm_ref[:, sl] = _rope_group(q[:, sl], cosq, sinq).astype(bf)
        km_ref[:, sl] = (kn[:, sl] + kr).astype(bf)
    vt_ref[...] = (vv + vone_ref[...]).T.astype(bf)


def _inproj(x2, nmix, win, qn, wuq, kvn, wuk, wuv, cosq, sinq, cosk, sink, vone, tm):
    t = x2.shape[0]
    n_tab = cosq.shape[0] // tm
    row = lambda i: (i, 0)
    col = lambda i: (0, i)
    tab = lambda i: (i % n_tab, 0)
    bf = jnp.bfloat16
    return pl.pallas_call(
        _inproj_kernel,
        grid=(t // tm,),
        in_specs=[
            pl.BlockSpec((tm, D_MODEL), row),
            _const_spec((1, D_MODEL)),
            _const_spec((D_MODEL, D_IN_EXT)),
            _const_spec((1, MLA_Q_RANK)),
            _const_spec((MLA_Q_RANK, MLA_QK)),
            _const_spec((1, MLA_KV_RANK)),
            _const_spec((MLA_KV_RANK, MLA_QK)),
            _const_spec((MLA_KV_RANK, MLA_QK)),
            pl.BlockSpec((tm, HEAD_GROUP), tab),
            pl.BlockSpec((tm, HEAD_GROUP), tab),
            pl.BlockSpec((tm, HEAD_GROUP), tab),
            pl.BlockSpec((tm, HEAD_GROUP), tab),
            _const_spec((1, MLA_QK)),
        ],
        out_specs=[
            pl.BlockSpec((tm, 2 * NA_W), row),
            pl.BlockSpec((NA_VT_ALL, tm), col),
            pl.BlockSpec((tm, MLA_QK), row),
            pl.BlockSpec((tm, MLA_QK), row),
            pl.BlockSpec((MLA_QK, tm), col),
            pl.BlockSpec((tm, 2 * D_MODEL), row),
        ],
        out_shape=[
            jax.ShapeDtypeStruct((t, 2 * NA_W), bf),
            jax.ShapeDtypeStruct((NA_VT_ALL, t), bf),
            jax.ShapeDtypeStruct((t, MLA_QK), bf),
            jax.ShapeDtypeStruct((t, MLA_QK), bf),
            jax.ShapeDtypeStruct((MLA_QK, t), bf),
            jax.ShapeDtypeStruct((t, 2 * D_MODEL), bf),
        ],
        compiler_params=pltpu.CompilerParams(
            dimension_semantics=("parallel",), vmem_limit_bytes=VMEM_LIMIT),
        name="inproj",
    )(x2, nmix, win, qn, wuq, kvn, wuk, wuv, cosq, sinq, cosk, sink, vone)


def _na_kernel(q_ref, k_ref, vt_ref, km_ref, vtm_ref, tbl_ref, o_ref, *, n_blocks):
    bf = jnp.bfloat16
    f32 = jnp.float32
    nb = q_ref.shape[0] // NA_Q
    rows = n_blocks * NA_ROWS_PER_BLOCK
    lane = lax.broadcasted_iota(jnp.int32, (NA_Q, LANES), 1)
    in_head = (lane < NA_HEAD_DIM, lane >= NA_HEAD_DIM)

    def window(i):
        j = pl.program_id(2) * nb + i
        ws = jnp.clip(j * NA_ROWS_PER_BLOCK - NA_WIN_H // 2, 0, rows - NA_KEY_ROWS)
        variant = jnp.where(j == 0, 0, jnp.where(j == n_blocks - 1, 2, 1))
        return pl.multiple_of(ws * GRID_W, NA_ROWS_PER_BLOCK * GRID_W), variant

    wins = [window(i) for i in range(nb)]
    s_val, p_val = {}, {}

    def scores(u):
        start, variant = wins[u]
        q2 = q_ref[u * NA_Q:(u + 1) * NA_Q, :]
        zero = jnp.zeros_like(q2)
        qq = jnp.concatenate([jnp.where(in_head[0], q2, zero), jnp.where(in_head[1], q2, zero)], axis=0)
        s_g = lax.dot_general(k_ref[pl.ds(start, NA_K), :], qq, _NT, preferred_element_type=f32)
        s_g = s_g + tbl_ref[variant, 0, :NA_K, :]
        s_m = lax.dot_general(km_ref[...], qq, _NT, preferred_element_type=f32)
        s_m = s_m + tbl_ref[variant, 0, NA_K:, :]
        s_val[u] = (s_g, s_m)
        mx = jnp.maximum(_fold_rows(s_g, jnp.max), _fold_rows(s_m, jnp.max))
        return jnp.max(mx, axis=0, keepdims=True)

    def probs(u, m):
        s_g, s_m = s_val.pop(u)
        p_val[u] = (jnp.exp2(s_g - m).astype(bf), jnp.exp2(s_m - m).astype(bf))

    def weighted(u):
        start = wins[u][0]
        p_g, p_m = p_val.pop(u)
        acc = (jnp.dot(vt_ref[:, pl.ds(start, NA_K)], p_g, preferred_element_type=f32)
               + jnp.dot(vtm_ref[...], p_m, preferred_element_type=f32))
        return acc[:2 * NA_HEAD_DIM] / acc[2 * NA_HEAD_DIM:2 * NA_HEAD_DIM + 1]

    col_max, out = {}, {}
    for k in range(nb + 2):
        if k < nb:
            col_max[k] = scores(k)
        if 1 <= k <= nb:
            probs(k - 1, col_max[k - 1])
        if k >= 2:
            out[k - 2] = weighted(k - 2)
    for i in range(nb):
        o_t = jnp.concatenate([out[i][:NA_HEAD_DIM, :NA_Q], out[i][NA_HEAD_DIM:, NA_Q:]], axis=0)
        o_ref[i * NA_Q:(i + 1) * NA_Q, :] = o_t.T.astype(bf)


def _na_attention(qk, vt, k_meta, vt_meta, tbl, batch, seq):
    n_blocks = seq // NA_Q
    n_steps = n_blocks // NA_BLOCKS_PER_STEP
    n_pairs = NA_HEADS // 2
    kcol = NA_W // LANES
    tq = NA_BLOCKS_PER_STEP * NA_Q
    return pl.pallas_call(
        functools.partial(_na_kernel, n_blocks=n_blocks),
        grid=(n_pairs, batch, n_steps),
        in_specs=[
            pl.BlockSpec((tq, LANES), lambda hp, b, j: (b * n_steps + j, hp)),
            pl.BlockSpec((seq, LANES), lambda hp, b, j: (b, kcol + hp)),
            pl.BlockSpec((NA_VT_ROWS, seq), lambda hp, b, j: (hp, b)),
            pl.BlockSpec((N_META, LANES), lambda hp, b, j: (0, kcol + hp)),
            pl.BlockSpec((NA_VT_ROWS, N_META), lambda hp, b, j: (hp, 0)),
            pl.BlockSpec((3, 1, NA_KT, 2 * NA_Q), lambda hp, b, j: (0, hp, 0, 0)),
        ],
        out_specs=pl.BlockSpec((tq, LANES), lambda hp, b, j: (b * n_steps + j, hp)),
        out_shape=jax.ShapeDtypeStruct((batch * seq, NA_W), jnp.bfloat16),
        compiler_params=pltpu.CompilerParams(
            dimension_semantics=("parallel", "parallel", "arbitrary"), vmem_limit_bytes=VMEM_LIMIT),
        name="na_attn",
    )(qk, qk, vt, k_meta, vt_meta, tbl)


def _na_bias_kernel(w_ref, o_ref, *, rows, n_blocks):
    kh = min(NA_WIN_H, rows)
    shape = (GRID_W, LANES)
    lane = lax.broadcasted_iota(jnp.int32, shape, 1)
    kc = lax.broadcasted_iota(jnp.int32, shape, 0)
    qc = lane % GRID_W
    cstart = jnp.clip(qc - NA_WIN_W // 2, 0, GRID_W - NA_WIN_W)
    cvalid = (kc >= cstart) & (kc < cstart + NA_WIN_W)
    neg = jnp.full(shape, NEG, jnp.float32)
    halves = {}

    def toeplitz(hh, dr, half):
        if (hh, dr, half) not in halves:
            row = jnp.broadcast_to(w_ref[hh, dr:dr + 1, :], shape)
            halves[hh, dr, half] = pltpu.roll(row, half * GRID_W, 1, stride=1, stride_axis=0)
        return halves[hh, dr, half]

    for variant, j in enumerate((0, 1, n_blocks - 1)):
        r0 = j * NA_ROWS_PER_BLOCK
        ws = min(max(r0 - NA_WIN_H // 2, 0), rows - NA_KEY_ROWS)
        o_ref[variant, 0, NA_K:, :] = jnp.zeros((N_META, 2 * NA_Q), jnp.float32)
        for hh in range(2):
            for rq_pair in range(NA_ROWS_PER_BLOCK // 2):
                for rk in range(NA_KEY_ROWS):
                    krow = ws + rk
                    tiles = []
                    for half in range(2):
                        r = r0 + 2 * rq_pair + half
                        rs = min(max(r - kh // 2, 0), rows - kh)
                        ok = rs <= krow < rs + kh
                        tiles.append(toeplitz(hh, krow - r + NA_WIN_H - 1, half) if ok else neg)
                    tile = jnp.where(cvalid, jnp.where(lane < GRID_W, tiles[0], tiles[1]), NEG)
                    lane0 = hh * NA_Q + rq_pair * LANES
                    o_ref[variant, 0, rk * GRID_W:(rk + 1) * GRID_W, lane0:lane0 + LANES] = tile


def _na_bias_table(rpb, seq):
    rows = seq // GRID_W
    n_blocks = rows // NA_ROWS_PER_BLOCK
    n_dr = 2 * NA_WIN_H - 1
    n_pairs = NA_HEADS // 2
    rpb = rpb.astype(jnp.float32) * LOG2E
    side = GRID_W - NA_WIN_W
    rpb_pad = jnp.pad(rpb, ((0, 0), (0, 0), (side, side)))
    w = jnp.concatenate([rpb_pad[..., GRID_W - 1:], jnp.zeros(rpb.shape[:2] + (1,), jnp.float32),
                         rpb_pad[..., :GRID_W - 1]], axis=-1)
    w = jnp.roll(jnp.flip(w, axis=-1), 1, axis=-1)
    return pl.pallas_call(
        functools.partial(_na_bias_kernel, rows=rows, n_blocks=n_blocks),
        grid=(n_pairs,),
        in_specs=[pl.BlockSpec((2, n_dr, LANES), lambda hp: (hp, 0, 0))],
        out_specs=pl.BlockSpec((3, 1, NA_KT, 2 * NA_Q), lambda hp: (0, hp, 0, 0)),
        out_shape=jax.ShapeDtypeStruct((3, n_pairs, NA_KT, 2 * NA_Q), jnp.float32),
        compiler_params=pltpu.CompilerParams(dimension_semantics=("parallel",), vmem_limit_bytes=VMEM_LIMIT),
        name="na_bias",
    )(w)


def _mla_kernel(q_ref, k_ref, vt_ref, km_ref, vtm_ref, o_ref, s_ref, p_ref):
    bf = jnp.bfloat16
    f32 = jnp.float32
    n_sub = q_ref.shape[0] // MLA_SUB
    n_chunks = k_ref.shape[0] // MLA_TK
    head = (slice(0, HEAD_GROUP), slice(HEAD_GROUP, 2 * HEAD_GROUP))
    chunk = [slice(c * MLA_TK, (c + 1) * MLA_TK) for c in range(n_chunks)]
    units = [(slice(i * MLA_SUB, (i + 1) * MLA_SUB), hh) for i in range(n_sub) for hh in range(2)]
    n_units = len(units)

    def meta_scores(u):
        rows, hh = units[u]
        return lax.dot_general(km_ref[:, head[hh]], q_ref[rows, head[hh]], _NT, preferred_element_type=f32)

    def scores(u, c):
        rows, hh = units[u]
        s = lax.dot_general(k_ref[chunk[c], head[hh]], q_ref[rows, head[hh]], _NT, preferred_element_type=f32)
        s_ref[u % 2, chunk[c], :] = s
        return _fold_rows(s, jnp.max)

    def probs(u, c, m):
        p_ref[u % 2, chunk[c], :] = jnp.exp2(s_ref[u % 2, chunk[c], :] - m).astype(bf)

    def weighted(u, c):
        hh = units[u][1]
        return jnp.dot(vt_ref[head[hh], chunk[c]], p_ref[u % 2, chunk[c], :], preferred_element_type=f32)

    s_meta, col_max, acc = {}, {}, {}
    for k in range(n_units + 2):
        if k < n_units:
            s_meta[k] = meta_scores(k)
            mx = _fold_rows(s_meta[k], jnp.max)
        if k >= 2:
            p_meta = jnp.exp2(s_meta[k - 2] - col_max[k - 2]).astype(bf)
            acc[k - 2] = jnp.dot(vtm_ref[head[units[k - 2][1]], :], p_meta, preferred_element_type=f32)
        for c in range(n_chunks):
            if k < n_units:
                mx = jnp.maximum(mx, scores(k, c))
            if 1 <= k <= n_units:
                probs(k - 1, c, col_max[k - 1])
            if k >= 2:
                acc[k - 2] = acc[k - 2] + weighted(k - 2, c)
        if k < n_units:
            col_max[k] = jnp.max(mx, axis=0, keepdims=True)

    for i in range(n_sub):
        o0, o1 = [(acc[2 * i + hh] / acc[2 * i + hh][MLA_V_DIM:MLA_V_DIM + 1, :])[:MLA_V_DIM] for hh in range(2)]
        o_ref[units[2 * i][0], :] = jnp.concatenate([o0, o1], axis=0).T.astype(bf)


def _mla_attention(qm, km, vt, km_meta, vt_meta, batch, seq, tq):
    nq = seq // tq
    n_pairs = MLA_HEADS // 2
    pair = 2 * HEAD_GROUP
    return pl.pallas_call(
        _mla_kernel,
        grid=(batch, n_pairs, nq),
        in_specs=[
            pl.BlockSpec((tq, pair), lambda b, hp, i: (b * nq + i, hp)),
            pl.BlockSpec((seq, pair), lambda b, hp, i: (b, hp)),
            pl.BlockSpec((pair, seq), lambda b, hp, i: (hp, b)),
            pl.BlockSpec((N_META, pair), lambda b, hp, i: (0, hp)),
            pl.BlockSpec((pair, N_META), lambda b, hp, i: (hp, 0)),
        ],
        out_specs=pl.BlockSpec((tq, 2 * MLA_V_DIM), lambda b, hp, i: (b * nq + i, hp)),
        out_shape=jax.ShapeDtypeStruct((batch * seq, MLA_HEADS * MLA_V_DIM), jnp.bfloat16),
        scratch_shapes=[pltpu.VMEM((2, seq, MLA_SUB), jnp.float32), pltpu.VMEM((2, seq, MLA_SUB), jnp.bfloat16)],
        compiler_params=pltpu.CompilerParams(
            dimension_semantics=("parallel", "parallel", "arbitrary"), vmem_limit_bytes=VMEM_LIMIT),
        name="mla_attn",
    )(qm, km, vt, km_meta, vt_meta)


def _post_kernel(x_ref, ona_ref, omla_ref, g_ref, wna_ref, wmla_ref, wout_ref, nffn_ref,
                 wff1_ref, wff2_ref, nfin_ref, o_ref):
    bf = jnp.bfloat16
    f32 = jnp.float32
    a = jnp.dot(ona_ref[...], wna_ref[...], preferred_element_type=f32)
    b = jnp.dot(omla_ref[...], wmla_ref[...], preferred_element_type=f32)
    g_na = g_ref[:, :D_MODEL].astype(f32)
    g_mla = g_ref[:, D_MODEL:].astype(f32)
    merged = jax.nn.sigmoid(g_na) * a + jax.nn.sigmoid(g_mla) * b
    h = x_ref[...] + jnp.dot(merged.astype(bf), wout_ref[...], preferred_element_type=f32)
    fn = _rms(h, nffn_ref[...]).astype(bf)
    u = jnp.dot(fn, wff1_ref[...], preferred_element_type=f32)
    u = jnp.square(jnp.maximum(u, 0.0)).astype(bf)
    h = h + jnp.dot(u, wff2_ref[...], preferred_element_type=f32)
    o_ref[...] = _rms(h, nfin_ref[...])


def _post(x2, o_na, o_mla, g, wna, wmla, wout, nffn, wff1, wff2, nfin, tm):
    t = x2.shape[0]
    row = lambda i: (i, 0)
    return pl.pallas_call(
        _post_kernel,
        grid=(t // tm,),
        in_specs=[
            pl.BlockSpec((tm, D_MODEL), row),
            pl.BlockSpec((tm, NA_W), row),
            pl.BlockSpec((tm, MLA_HEADS * MLA_V_DIM), row),
            pl.BlockSpec((tm, 2 * D_MODEL), row),
            _const_spec((NA_W, D_MODEL)),
            _const_spec((MLA_HEADS * MLA_V_DIM, D_MODEL)),
            _const_spec((D_MODEL, D_MODEL)),
            _const_spec((1, D_MODEL)),
            _const_spec((D_MODEL, D_FF)),
            _const_spec((D_FF, D_MODEL)),
            _const_spec((1, D_MODEL)),
        ],
        out_specs=pl.BlockSpec((tm, D_MODEL), row),
        out_shape=jax.ShapeDtypeStruct((t, D_MODEL), jnp.float32),
        compiler_params=pltpu.CompilerParams(
            dimension_semantics=("parallel",), vmem_limit_bytes=VMEM_LIMIT),
        name="post",
    )(x2, o_na, o_mla, g, wna, wmla, wout, nffn, wff1, wff2, nfin)


def _head_groups(w, per_head, lane0):
    k = w.shape[0]
    w = w.reshape(k, MLA_HEADS, per_head)
    w = jnp.pad(w, ((0, 0), (0, 0), (lane0, HEAD_GROUP - lane0 - per_head)))
    return w.reshape(k, MLA_HEADS * HEAD_GROUP)


def _rope_tables(pos0, n, scale):
    pos = np.arange(pos0, pos0 + n, dtype=np.float64)
    inv_freq = 1.0 / (ROPE_THETA ** (np.arange(0, MLA_ROPE_DIM, 2, dtype=np.float64) / MLA_ROPE_DIM))
    ang = pos[:, None] * inv_freq[None, :]
    cos, sin = np.cos(ang), np.sin(ang)
    ones = np.ones((n, ROPE_LANE0))
    tail = HEAD_GROUP - ROPE_LANE0 - MLA_ROPE_DIM
    cos_t = np.concatenate([ones, cos, cos, np.ones((n, tail))], axis=1) * scale
    sin_t = np.concatenate([0 * ones, -sin, sin, np.zeros((n, tail))], axis=1) * scale
    return jnp.asarray(cos_t, jnp.float32), jnp.asarray(sin_t, jnp.float32)


def kernel(x, meta, norm_mix, w_in, na_rpb, mla_q_norm, w_uq, mla_kv_norm, w_ukv, w_na_out, w_mla_out,
           w_out, norm_ffn, w_ff1, w_ff2, norm_final):
    assert norm_mix.shape[0] == 1, "single-layer block"
    batch, seq, d = x.shape
    bf = jnp.bfloat16
    f32 = jnp.float32
    x2 = x.reshape(batch * seq, d)

    wi = w_in[0]
    kr_cols = jnp.pad(wi[:, C_KR:C_KR + MLA_ROPE_DIM],
                      ((0, 0), (ROPE_LANE0, HEAD_GROUP - ROPE_LANE0 - MLA_ROPE_DIM)))
    win = jnp.concatenate([wi[:, :C_KR], kr_cols, wi[:, C_KR + MLA_ROPE_DIM:]], axis=1).astype(bf)
    wuq = _head_groups(w_uq[0], MLA_NOPE_DIM + MLA_ROPE_DIM, 0).astype(bf)
    wukv = w_ukv[0].reshape(MLA_KV_RANK, MLA_HEADS, MLA_NOPE_DIM + MLA_V_DIM)
    wuk = _head_groups(wukv[:, :, :MLA_NOPE_DIM].reshape(MLA_KV_RANK, -1), MLA_NOPE_DIM, 0).astype(bf)
    wuv = _head_groups(wukv[:, :, MLA_NOPE_DIM:].reshape(MLA_KV_RANK, -1), MLA_V_DIM, 0).astype(bf)
    vone = (jnp.arange(MLA_QK) % HEAD_GROUP == MLA_V_DIM).astype(f32)[None]

    mla_scale = (MLA_NOPE_DIM + MLA_ROPE_DIM) ** -0.5 * LOG2E
    cosq, sinq = _rope_tables(N_META, seq, mla_scale)
    cosk, sink = _rope_tables(N_META, seq, 1.0)
    cosq_m, sinq_m = _rope_tables(0, META_PAD, mla_scale)
    cosk_m, sink_m = _rope_tables(0, META_PAD, 1.0)

    small = (norm_mix[0][None], win, mla_q_norm[0][None], wuq, mla_kv_norm[0][None], wuk, wuv)
    qk, vnat, qm, km, vt, g = _inproj(x2, *small, cosq, sinq, cosk, sink, vone, tm=512)
    meta_pad = jnp.pad(meta.astype(f32), ((0, META_PAD - N_META), (0, 0)))
    qk_meta, vnat_meta, _, km_meta, vt_meta, _ = _inproj(meta_pad, *small, cosq_m, sinq_m, cosk_m, sink_m, vone,
                                                         tm=META_PAD)

    tbl = _na_bias_table(na_rpb[0], seq)
    o_na = _na_attention(qk, vnat, qk_meta[:N_META], vnat_meta[:, :N_META], tbl, batch, seq)
    o_mla = _mla_attention(qm, km, vt, km_meta[:N_META], vt_meta[:, :N_META], batch, seq, tq=2048)

    out = _post(x2, o_na, o_mla, g, w_na_out[0].astype(bf), w_mla_out[0].astype(bf), w_out[0].astype(bf),
                norm_ffn[0][None], w_ff1[0].astype(bf), w_ff2[0].astype(bf), norm_final[None], tm=512)
    return out.reshape(batch, seq, d)
```

```python
import functools

import numpy as np
import jax
import jax.numpy as jnp
from jax import lax
from jax.experimental import pallas as pl
from jax.experimental.pallas import tpu as pltpu

D_MODEL = 1024
N_META = 16
GRID_W = 64
NA_HEADS = 8
NA_HEAD_DIM = 64
NA_WIN_H = 8
NA_WIN_W = 16
NA_W = NA_HEADS * NA_HEAD_DIM
MLA_HEADS = 8
MLA_NOPE_DIM = 64
MLA_ROPE_DIM = 32
MLA_V_DIM = 64
MLA_Q_RANK = 384
MLA_KV_RANK = 256
ROPE_THETA = 10000.0
D_FF = 4 * D_MODEL
EPS = 1e-6

LANES = 128
SUBLANES = 8
HEAD_GROUP = 128
ROPE_LANE0 = MLA_NOPE_DIM
MLA_QK = MLA_HEADS * HEAD_GROUP
NEG = -1e30

C_QKV = 0
C_CQ = 3 * NA_W
C_CKV = C_CQ + MLA_Q_RANK
C_KR = C_CKV + MLA_KV_RANK
C_G = C_KR + HEAD_GROUP
D_IN_EXT = C_G + 2 * D_MODEL

NA_ROWS_PER_BLOCK = 4
NA_KEY_ROWS = NA_ROWS_PER_BLOCK + NA_WIN_H
NA_Q = NA_ROWS_PER_BLOCK * GRID_W
NA_K = NA_KEY_ROWS * GRID_W
NA_KT = NA_K + N_META
NA_VT_ROWS = LANES + 16
NA_VT_ALL = (NA_HEADS // 2) * NA_VT_ROWS
NA_BLOCKS_PER_STEP = 16
META_PAD = LANES
LOG2E = float(np.log2(np.e))
MLA_TK = 512
MLA_SUB = 512

VMEM_LIMIT = 56 * 1024 * 1024

_NT = (((1,), (1,)), ((), ()))


def _const_spec(shape):
    nd = len(shape)
    return pl.BlockSpec(shape, lambda *_: (0,) * nd, pipeline_mode=pl.Buffered(1))


def _rms(x, g):
    return x * lax.rsqrt(jnp.mean(x * x, axis=-1, keepdims=True) + EPS) * g


def _rope_group(x, cos, sin):
    lane = lax.broadcasted_iota(jnp.int32, x.shape, 1)
    partner = jnp.where(lane < ROPE_LANE0 + MLA_ROPE_DIM // 2,
                        pltpu.roll(x, HEAD_GROUP - MLA_ROPE_DIM // 2, 1),
                        pltpu.roll(x, MLA_ROPE_DIM // 2, 1))
    return x * cos + partner * sin


def _fold_rows(x, op):
    return op(x.reshape(x.shape[0] // SUBLANES, SUBLANES, x.shape[1]), axis=0)


def _inproj_kernel(x_ref, nmix_ref, win_ref, qn_ref, wuq_ref, kvn_ref, wuk_ref, wuv_ref,
                   cosq_ref, sinq_ref, cosk_ref, sink_ref, vone_ref,
                   qk_ref, vnat_ref, qm_ref, km_ref, vt_ref, g_ref):
    bf = jnp.bfloat16
    f32 = jnp.float32
    hn = _rms(x_ref[...], nmix_ref[...]).astype(bf)

    lat = jnp.dot(hn, win_ref[:, C_CQ:C_G], preferred_element_type=f32)
    cq = _rms(lat[:, :MLA_Q_RANK], qn_ref[...]).astype(bf)
    ckv = _rms(lat[:, MLA_Q_RANK:MLA_Q_RANK + MLA_KV_RANK], kvn_ref[...]).astype(bf)
    kr = lat[:, MLA_Q_RANK + MLA_KV_RANK:]
    kr = _rope_group(kr, cosk_ref[...], sink_ref[...])

    qkv = jnp.dot(hn, win_ref[:, C_QKV:C_CQ], preferred_element_type=f32)
    qk_ref[:, :NA_W] = (qkv[:, :NA_W] * (NA_HEAD_DIM ** -0.5 * LOG2E)).astype(bf)
    qk_ref[:, NA_W:] = qkv[:, NA_W:2 * NA_W].astype(bf)
    v_t = qkv[:, 2 * NA_W:].T.astype(bf)
    row = lax.broadcasted_iota(jnp.int32, (NA_VT_ROWS - LANES, v_t.shape[1]), 0)
    ones_rows = jnp.where(row == 0, 1.0, 0.0).astype(bf)
    for hp in range(NA_HEADS // 2):
        vnat_ref[hp * NA_VT_ROWS:hp * NA_VT_ROWS + LANES, :] = v_t[hp * LANES:(hp + 1) * LANES]
        vnat_ref[hp * NA_VT_ROWS + LANES:(hp + 1) * NA_VT_ROWS, :] = ones_rows

    q = jnp.dot(cq, wuq_ref[...], preferred_element_type=f32)
    kn = jnp.dot(ckv, wuk_ref[...], preferred_element_type=f32)
    vv = jnp.dot(ckv, wuv_ref[...], preferred_element_type=f32)

    g_ref[...] = jnp.dot(hn, win_ref[:, C_G:], preferred_element_type=f32).astype(bf)

    cosq = cosq_ref[...]
    sinq = sinq_ref[...]
    for h in range(MLA_HEADS):
        sl = slice(h * HEAD_GROUP, (h + 1) * HEAD_GROUP)
        qm_ref[:, sl] = _rope_group(q[:, sl], cosq, sinq).astype(bf)
        km_ref[:, sl] = (kn[:, sl] + kr).astype(bf)
    vt_ref[...] = (vv + vone_ref[...]).T.astype(bf)


def _inproj(x2, nmix, win, qn, wuq, kvn, wuk, wuv, cosq, sinq, cosk, sink, vone, tm):
    t = x2.shape[0]
    n_tab = cosq.shape[0] // tm
    row = lambda i: (i, 0)
    col = lambda i: (0, i)
    tab = lambda i: (i % n_tab, 0)
    bf = jnp.bfloat16
    return pl.pallas_call(
        _inproj_kernel,
        grid=(t // tm,),
        in_specs=[
            pl.BlockSpec((tm, D_MODEL), row),
            _const_spec((1, D_MODEL)),
            _const_spec((D_MODEL, D_IN_EXT)),
            _const_spec((1, MLA_Q_RANK)),
            _const_spec((MLA_Q_RANK, MLA_QK)),
            _const_spec((1, MLA_KV_RANK)),
            _const_spec((MLA_KV_RANK, MLA_QK)),
            _const_spec((MLA_KV_RANK, MLA_QK)),
            pl.BlockSpec((tm, HEAD_GROUP), tab),
            pl.BlockSpec((tm, HEAD_GROUP), tab),
            pl.BlockSpec((tm, HEAD_GROUP), tab),
            pl.BlockSpec((tm, HEAD_GROUP), tab),
            _const_spec((1, MLA_QK)),
        ],
        out_specs=[
            pl.BlockSpec((tm, 2 * NA_W), row),
            pl.BlockSpec((NA_VT_ALL, tm), col),
            pl.BlockSpec((tm, MLA_QK), row),
            pl.BlockSpec((tm, MLA_QK), row),
            pl.BlockSpec((MLA_QK, tm), col),
            pl.BlockSpec((tm, 2 * D_MODEL), row),
        ],
        out_shape=[
            jax.ShapeDtypeStruct((t, 2 * NA_W), bf),
            jax.ShapeDtypeStruct((NA_VT_ALL, t), bf),
            jax.ShapeDtypeStruct((t, MLA_QK), bf),
            jax.ShapeDtypeStruct((t, MLA_QK), bf),
            jax.ShapeDtypeStruct((MLA_QK, t), bf),
            jax.ShapeDtypeStruct((t, 2 * D_MODEL), bf),
        ],
        compiler_params=pltpu.CompilerParams(
            dimension_semantics=("parallel",), vmem_limit_bytes=VMEM_LIMIT),
        name="inproj",
    )(x2, nmix, win, qn, wuq, kvn, wuk, wuv, cosq, sinq, cosk, sink, vone)


def _na_kernel(q_ref, k_ref, vt_ref, km_ref, vtm_ref, tbl_ref, o_ref, *, n_blocks):
    bf = jnp.bfloat16
    f32 = jnp.float32
    nb = q_ref.shape[0] // NA_Q
    rows = n_blocks * NA_ROWS_PER_BLOCK
    lane = lax.broadcasted_iota(jnp.int32, (NA_Q, LANES), 1)
    in_head = (lane < NA_HEAD_DIM, lane >= NA_HEAD_DIM)

    def window(i):
        j = pl.program_id(2) * nb + i
        ws = jnp.clip(j * NA_ROWS_PER_BLOCK - NA_WIN_H // 2, 0, rows - NA_KEY_ROWS)
        variant = jnp.where(j == 0, 0, jnp.where(j == n_blocks - 1, 2, 1))
        return pl.multiple_of(ws * GRID_W, NA_ROWS_PER_BLOCK * GRID_W), variant

    wins = [window(i) for i in range(nb)]
    s_val, p_val = {}, {}

    def scores(u):
        start, variant = wins[u]
        q2 = q_ref[u * NA_Q:(u + 1) * NA_Q, :]
        zero = jnp.zeros_like(q2)
        qq = jnp.concatenate([jnp.where(in_head[0], q2, zero), jnp.where(in_head[1], q2, zero)], axis=0)
        s_g = lax.dot_general(k_ref[pl.ds(start, NA_K), :], qq, _NT, preferred_element_type=f32)
        s_g = s_g + tbl_ref[variant, 0, :NA_K, :]
        s_m = lax.dot_general(km_ref[...], qq, _NT, preferred_element_type=f32)
        s_m = s_m + tbl_ref[variant, 0, NA_K:, :]
        s_val[u] = (s_g, s_m)
        mx = jnp.maximum(_fold_rows(s_g, jnp.max), _fold_rows(s_m, jnp.max))
        return jnp.max(mx, axis=0, keepdims=True)

    def probs(u, m):
        s_g, s_m = s_val.pop(u)
        p_val[u] = (jnp.exp2(s_g - m).astype(bf), jnp.exp2(s_m - m).astype(bf))

    def weighted(u):
        start = wins[u][0]
        p_g, p_m = p_val.pop(u)
        acc = (jnp.dot(vt_ref[:, pl.ds(start, NA_K)], p_g, preferred_element_type=f32)
               + jnp.dot(vtm_ref[...], p_m, preferred_element_type=f32))
        return acc[:2 * NA_HEAD_DIM] / acc[2 * NA_HEAD_DIM:2 * NA_HEAD_DIM + 1]

    col_max, out = {}, {}
    for k in range(nb + 2):
        if k < nb:
            col_max[k] = scores(k)
        if 1 <= k <= nb:
            probs(k - 1, col_max[k - 1])
        if k >= 2:
            out[k - 2] = weighted(k - 2)
    for i in range(nb):
        o_t = jnp.concatenate([out[i][:NA_HEAD_DIM, :NA_Q], out[i][NA_HEAD_DIM:, NA_Q:]], axis=0)
        o_ref[i * NA_Q:(i + 1) * NA_Q, :] = o_t.T.astype(bf)


def _na_attention(qk, vt, k_meta, vt_meta, tbl, batch, seq):
    n_blocks = seq // NA_Q
    n_steps = n_blocks // NA_BLOCKS_PER_STEP
    n_pairs = NA_HEADS // 2
    kcol = NA_W // LANES
    tq = NA_BLOCKS_PER_STEP * NA_Q
    return pl.pallas_call(
        functools.partial(_na_kernel, n_blocks=n_blocks),
        grid=(n_pairs, batch, n_steps),
        in_specs=[
            pl.BlockSpec((tq, LANES), lambda hp, b, j: (b * n_steps + j, hp)),
            pl.BlockSpec((seq, LANES), lambda hp, b, j: (b, kcol + hp)),
            pl.BlockSpec((NA_VT_ROWS, seq), lambda hp, b, j: (hp, b)),
            pl.BlockSpec((N_META, LANES), lambda hp, b, j: (0, kcol + hp)),
            pl.BlockSpec((NA_VT_ROWS, N_META), lambda hp, b, j: (hp, 0)),
            pl.BlockSpec((3, 1, NA_KT, 2 * NA_Q), lambda hp, b, j: (0, hp, 0, 0)),
        ],
        out_specs=pl.BlockSpec((tq, LANES), lambda hp, b, j: (b * n_steps + j, hp)),
        out_shape=jax.ShapeDtypeStruct((batch * seq, NA_W), jnp.bfloat16),
        compiler_params=pltpu.CompilerParams(
            dimension_semantics=("parallel", "parallel", "arbitrary"), vmem_limit_bytes=VMEM_LIMIT),
        name="na_attn",
    )(qk, qk, vt, k_meta, vt_meta, tbl)


def _na_bias_kernel(w_ref, o_ref, *, rows, n_blocks):
    kh = min(NA_WIN_H, rows)
    shape = (GRID_W, LANES)
    lane = lax.broadcasted_iota(jnp.int32, shape, 1)
    kc = lax.broadcasted_iota(jnp.int32, shape, 0)
    qc = lane % GRID_W
    cstart = jnp.clip(qc - NA_WIN_W // 2, 0, GRID_W - NA_WIN_W)
    cvalid = (kc >= cstart) & (kc < cstart + NA_WIN_W)
    neg = jnp.full(shape, NEG, jnp.float32)
    halves = {}

    def toeplitz(hh, dr, half):
        if (hh, dr, half) not in halves:
            row = jnp.broadcast_to(w_ref[hh, dr:dr + 1, :], shape)
            halves[hh, dr, half] = pltpu.roll(row, half * GRID_W, 1, stride=1, stride_axis=0)
        return halves[hh, dr, half]

    for variant, j in enumerate((0, 1, n_blocks - 1)):
        r0 = j * NA_ROWS_PER_BLOCK
        ws = min(max(r0 - NA_WIN_H // 2, 0), rows - NA_KEY_ROWS)
        o_ref[variant, 0, NA_K:, :] = jnp.zeros((N_META, 2 * NA_Q), jnp.float32)
        for hh in range(2):
            for rq_pair in range(NA_ROWS_PER_BLOCK // 2):
                for rk in range(NA_KEY_ROWS):
                    krow = ws + rk
                    tiles = []
                    for half in range(2):
                        r = r0 + 2 * rq_pair + half
                        rs = min(max(r - kh // 2, 0), rows - kh)
                        ok = rs <= krow < rs + kh
                        tiles.append(toeplitz(hh, krow - r + NA_WIN_H - 1, half) if ok else neg)
                    tile = jnp.where(cvalid, jnp.where(lane < GRID_W, tiles[0], tiles[1]), NEG)
                    lane0 = hh * NA_Q + rq_pair * LANES
                    o_ref[variant, 0, rk * GRID_W:(rk + 1) * GRID_W, lane0:lane0 + LANES] = tile


def _na_bias_table(rpb, seq):
    rows = seq // GRID_W
    n_blocks = rows // NA_ROWS_PER_BLOCK
    n_dr = 2 * NA_WIN_H - 1
    n_pairs = NA_HEADS // 2
    rpb = rpb.astype(jnp.float32) * LOG2E
    side = GRID_W - NA_WIN_W
    rpb_pad = jnp.pad(rpb, ((0, 0), (0, 0), (side, side)))
    w = jnp.concatenate([rpb_pad[..., GRID_W - 1:], jnp.zeros(rpb.shape[:2] + (1,), jnp.float32),
                         rpb_pad[..., :GRID_W - 1]], axis=-1)
    w = jnp.roll(jnp.flip(w, axis=-1), 1, axis=-1)
    return pl.pallas_call(
        functools.partial(_na_bias_kernel, rows=rows, n_blocks=n_blocks),
        grid=(n_pairs,),
        in_specs=[pl.BlockSpec((2, n_dr, LANES), lambda hp: (hp, 0, 0))],
        out_specs=pl.BlockSpec((3, 1, NA_KT, 2 * NA_Q), lambda hp: (0, hp, 0, 0)),
        out_shape=jax.ShapeDtypeStruct((3, n_pairs, NA_KT, 2 * NA_Q), jnp.float32),
        compiler_params=pltpu.CompilerParams(dimension_semantics=("parallel",), vmem_limit_bytes=VMEM_LIMIT),
        name="na_bias",
    )(w)


def _mla_kernel(q_ref, k_ref, vt_ref, km_ref, vtm_ref, o_ref, s_ref, p_ref):
    bf = jnp.bfloat16
    f32 = jnp.float32
    n_sub = q_ref.shape[0] // MLA_SUB
    n_chunks = k_ref.shape[0] // MLA_TK
    head = (slice(0, HEAD_GROUP), slice(HEAD_GROUP, 2 * HEAD_GROUP))
    chunk = [slice(c * MLA_TK, (c + 1) * MLA_TK) for c in range(n_chunks)]
    units = [(slice(i * MLA_SUB, (i + 1) * MLA_SUB), hh) for i in range(n_sub) for hh in range(2)]
    n_units = len(units)

    def meta_scores(u):
        rows, hh = units[u]
        return lax.dot_general(km_ref[:, head[hh]], q_ref[rows, head[hh]], _NT, preferred_element_type=f32)

    def scores(u, c):
        rows, hh = units[u]
        s = lax.dot_general(k_ref[chunk[c], head[hh]], q_ref[rows, head[hh]], _NT, preferred_element_type=f32)
        s_ref[u % 2, chunk[c], :] = s
        return _fold_rows(s, jnp.max)

    def probs(u, c, m):
        p_ref[u % 2, chunk[c], :] = jnp.exp2(s_ref[u % 2, chunk[c], :] - m).astype(bf)

    def weighted(u, c):
        hh = units[u][1]
        return jnp.dot(vt_ref[head[hh], chunk[c]], p_ref[u % 2, chunk[c], :], preferred_element_type=f32)

    s_meta, col_max, acc = {}, {}, {}
    for k in range(n_units + 2):
        if k < n_units:
            s_meta[k] = meta_scores(k)
            mx = _fold_rows(s_meta[k], jnp.max)
        if k >= 2:
            p_meta = jnp.exp2(s_meta[k - 2] - col_max[k - 2]).astype(bf)
            acc[k - 2] = jnp.dot(vtm_ref[head[units[k - 2][1]], :], p_meta, preferred_element_type=f32)
        for c in range(n_chunks):
            if k < n_units:
                mx = jnp.maximum(mx, scores(k, c))
            if 1 <= k <= n_units:
                probs(k - 1, c, col_max[k - 1])
            if k >= 2:
                acc[k - 2] = acc[k - 2] + weighted(k - 2, c)
        if k < n_units:
            col_max[k] = jnp.max(mx, axis=0, keepdims=True)

    for i in range(n_sub):
        o0, o1 = [(acc[2 * i + hh] / acc[2 * i + hh][MLA_V_DIM:MLA_V_DIM + 1, :])[:MLA_V_DIM] for hh in range(2)]
        o_ref[units[2 * i][0], :] = jnp.concatenate([o0, o1], axis=0).T.astype(bf)


def _mla_attention(qm, km, vt, km_meta, vt_meta, batch, seq, tq):
    nq = seq // tq
    n_pairs = MLA_HEADS // 2
    pair = 2 * HEAD_GROUP
    return pl.pallas_call(
        _mla_kernel,
        grid=(batch, n_pairs, nq),
        in_specs=[
            pl.BlockSpec((tq, pair), lambda b, hp, i: (b * nq + i, hp)),
            pl.BlockSpec((seq, pair), lambda b, hp, i: (b, hp)),
            pl.BlockSpec((pair, seq), lambda b, hp, i: (hp, b)),
            pl.BlockSpec((N_META, pair), lambda b, hp, i: (0, hp)),
            pl.BlockSpec((pair, N_META), lambda b, hp, i: (hp, 0)),
        ],
        out_specs=pl.BlockSpec((tq, 2 * MLA_V_DIM), lambda b, hp, i: (b * nq + i, hp)),
        out_shape=jax.ShapeDtypeStruct((batch * seq, MLA_HEADS * MLA_V_DIM), jnp.bfloat16),
        scratch_shapes=[pltpu.VMEM((2, seq, MLA_SUB), jnp.float32), pltpu.VMEM((2, seq, MLA_SUB), jnp.bfloat16)],
        compiler_params=pltpu.CompilerParams(
            dimension_semantics=("parallel", "parallel", "arbitrary"), vmem_limit_bytes=VMEM_LIMIT),
        name="mla_attn",
    )(qm, km, vt, km_meta, vt_meta)


def _post_kernel(x_ref, ona_ref, omla_ref, g_ref, wna_ref, wmla_ref, wout_ref, nffn_ref,
                 wff1_ref, wff2_ref, nfin_ref, o_ref):
    bf = jnp.bfloat16
    f32 = jnp.float32
    a = jnp.dot(ona_ref[...], wna_ref[...], preferred_element_type=f32)
    b = jnp.dot(omla_ref[...], wmla_ref[...], preferred_element_type=f32)
    g_na = g_ref[:, :D_MODEL].astype(f32)
    g_mla = g_ref[:, D_MODEL:].astype(f32)
    merged = jax.nn.sigmoid(g_na) * a + jax.nn.sigmoid(g_mla) * b
    h = x_ref[...] + jnp.dot(merged.astype(bf), wout_ref[...], preferred_element_type=f32)
    fn = _rms(h, nffn_ref[...]).astype(bf)
    u = jnp.dot(fn, wff1_ref[...], preferred_element_type=f32)
    u = jnp.square(jnp.maximum(u, 0.0)).astype(bf)
    h = h + jnp.dot(u, wff2_ref[...], preferred_element_type=f32)
    o_ref[...] = _rms(h, nfin_ref[...])


def _post(x2, o_na, o_mla, g, wna, wmla, wout, nffn, wff1, wff2, nfin, tm):
    t = x2.shape[0]
    row = lambda i: (i, 0)
    return pl.pallas_call(
        _post_kernel,
        grid=(t // tm,),
        in_specs=[
            pl.BlockSpec((tm, D_MODEL), row),
            pl.BlockSpec((tm, NA_W), row),
            pl.BlockSpec((tm, MLA_HEADS * MLA_V_DIM), row),
            pl.BlockSpec((tm, 2 * D_MODEL), row),
            _const_spec((NA_W, D_MODEL)),
            _const_spec((MLA_HEADS * MLA_V_DIM, D_MODEL)),
            _const_spec((D_MODEL, D_MODEL)),
            _const_spec((1, D_MODEL)),
            _const_spec((D_MODEL, D_FF)),
            _const_spec((D_FF, D_MODEL)),
            _const_spec((1, D_MODEL)),
        ],
        out_specs=pl.BlockSpec((tm, D_MODEL), row),
        out_shape=jax.ShapeDtypeStruct((t, D_MODEL), jnp.float32),
        compiler_params=pltpu.CompilerParams(
            dimension_semantics=("parallel",), vmem_limit_bytes=VMEM_LIMIT),
        name="post",
    )(x2, o_na, o_mla, g, wna, wmla, wout, nffn, wff1, wff2, nfin)


def _head_groups(w, per_head, lane0):
    k = w.shape[0]
    w = w.reshape(k, MLA_HEADS, per_head)
    w = jnp.pad(w, ((0, 0), (0, 0), (lane0, HEAD_GROUP - lane0 - per_head)))
    return w.reshape(k, MLA_HEADS * HEAD_GROUP)


def _rope_tables(pos0, n, scale):
    pos = np.arange(pos0, pos0 + n, dtype=np.float64)
    inv_freq = 1.0 / (ROPE_THETA ** (np.arange(0, MLA_ROPE_DIM, 2, dtype=np.float64) / MLA_ROPE_DIM))
    ang = pos[:, None] * inv_freq[None, :]
    cos, sin = np.cos(ang), np.sin(ang)
    ones = np.ones((n, ROPE_LANE0))
    tail = HEAD_GROUP - ROPE_LANE0 - MLA_ROPE_DIM
    cos_t = np.concatenate([ones, cos, cos, np.ones((n, tail))], axis=1) * scale
    sin_t = np.concatenate([0 * ones, -sin, sin, np.zeros((n, tail))], axis=1) * scale
    return jnp.asarray(cos_t, jnp.float32), jnp.asarray(sin_t, jnp.float32)


def kernel(x, meta, norm_mix, w_in, na_rpb, mla_q_norm, w_uq, mla_kv_norm, w_ukv, w_na_out, w_mla_out,
           w_out, norm_ffn, w_ff1, w_ff2, norm_final):
    assert norm_mix.shape[0] == 1, "single-layer block"
    batch, seq, d = x.shape
    bf = jnp.bfloat16
    f32 = jnp.float32
    x2 = x.reshape(batch * seq, d)

    wi = w_in[0]
    kr_cols = jnp.pad(wi[:, C_KR:C_KR + MLA_ROPE_DIM],
                      ((0, 0), (ROPE_LANE0, HEAD_GROUP - ROPE_LANE0 - MLA_ROPE_DIM)))
    win = jnp.concatenate([wi[:, :C_KR], kr_cols, wi[:, C_KR + MLA_ROPE_DIM:]], axis=1).astype(bf)
    wuq = _head_groups(w_uq[0], MLA_NOPE_DIM + MLA_ROPE_DIM, 0).astype(bf)
    wukv = w_ukv[0].reshape(MLA_KV_RANK, MLA_HEADS, MLA_NOPE_DIM + MLA_V_DIM)
    wuk = _head_groups(wukv[:, :, :MLA_NOPE_DIM].reshape(MLA_KV_RANK, -1), MLA_NOPE_DIM, 0).astype(bf)
    wuv = _head_groups(wukv[:, :, MLA_NOPE_DIM:].reshape(MLA_KV_RANK, -1), MLA_V_DIM, 0).astype(bf)
    vone = (jnp.arange(MLA_QK) % HEAD_GROUP == MLA_V_DIM).astype(f32)[None]

    mla_scale = (MLA_NOPE_DIM + MLA_ROPE_DIM) ** -0.5 * LOG2E
    cosq, sinq = _rope_tables(N_META, seq, mla_scale)
    cosk, sink = _rope_tables(N_META, seq, 1.0)
    cosq_m, sinq_m = _rope_tables(0, META_PAD, mla_scale)
    cosk_m, sink_m = _rope_tables(0, META_PAD, 1.0)

    small = (norm_mix[0][None], win, mla_q_norm[0][None], wuq, mla_kv_norm[0][None], wuk, wuv)
    qk, vnat, qm, km, vt, g = _inproj(x2, *small, cosq, sinq, cosk, sink, vone, tm=512)
    meta_pad = jnp.pad(meta.astype(f32), ((0, META_PAD - N_META), (0, 0)))
    qk_meta, vnat_meta, _, km_meta, vt_meta, _ = _inproj(meta_pad, *small, cosq_m, sinq_m, cosk_m, sink_m, vone,
                                                         tm=META_PAD)

    tbl = _na_bias_table(na_rpb[0], seq)
    o_na = _na_attention(qk, vnat, qk_meta[:N_META], vnat_meta[:, :N_META], tbl, batch, seq)
    o_mla = _mla_attention(qm, km, vt, km_meta[:N_META], vt_meta[:, :N_META], batch, seq, tq=4096)

    out = _post(x2, o_na, o_mla, g, w_na_out[0].astype(bf), w_mla_out[0].astype(bf), w_out[0].astype(bf),
                norm_ffn[0][None], w_ff1[0].astype(bf), w_ff2[0].astype(bf), norm_final[None], tm=512)
    return out.reshape(batch, seq, d)
```

```python
import functools

import numpy as np
import jax
import jax.numpy as jnp
from jax import lax
from jax.experimental import pallas as pl
from jax.experimental.pallas import tpu as pltpu

D_MODEL = 1024
N_META = 16
GRID_W = 64
NA_HEADS = 8
NA_HEAD_DIM = 64
NA_WIN_H = 8
NA_WIN_W = 16
NA_W = NA_HEADS * NA_HEAD_DIM
MLA_HEADS = 8
MLA_NOPE_DIM = 64
MLA_ROPE_DIM = 32
MLA_V_DIM = 64
MLA_Q_RANK = 384
MLA_KV_RANK = 256
ROPE_THETA = 10000.0
D_FF = 4 * D_MODEL
EPS = 1e-6

LANES = 128
SUBLANES = 8
HEAD_GROUP = 128
ROPE_LANE0 = MLA_NOPE_DIM
MLA_QK = MLA_HEADS * HEAD_GROUP
NEG = -1e30

C_CQ = 3 * NA_W
C_CKV = C_CQ + MLA_Q_RANK
C_KR = C_CKV + MLA_KV_RANK
C_G = C_KR + HEAD_GROUP

NA_ROWS_PER_BLOCK = 4
NA_KEY_ROWS = NA_ROWS_PER_BLOCK + NA_WIN_H
NA_Q = NA_ROWS_PER_BLOCK * GRID_W
NA_K = NA_KEY_ROWS * GRID_W
NA_KT = NA_K + N_META
NA_BLOCKS_PER_STEP = 16
META_PAD = LANES
LOG2E = float(np.log2(np.e))
MLA_TK = 512
MLA_SUB = 512

VMEM_LIMIT = 56 * 1024 * 1024

_NT = (((1,), (1,)), ((), ()))


def _const_spec(shape):
    nd = len(shape)
    return pl.BlockSpec(shape, lambda *_: (0,) * nd, pipeline_mode=pl.Buffered(1))


def _rms(x, g):
    return x * lax.rsqrt(jnp.mean(x * x, axis=-1, keepdims=True) + EPS) * g


def _rope_group(x, cos, sin):
    lane = lax.broadcasted_iota(jnp.int32, x.shape, 1)
    partner = jnp.where(lane < ROPE_LANE0 + MLA_ROPE_DIM // 2,
                        pltpu.roll(x, HEAD_GROUP - MLA_ROPE_DIM // 2, 1),
                        pltpu.roll(x, MLA_ROPE_DIM // 2, 1))
    return x * cos + partner * sin


def _fold_rows(x, op):
    return op(x.reshape(x.shape[0] // SUBLANES, SUBLANES, x.shape[1]), axis=0)


def _inproj_kernel(x_ref, nmix_ref, wqkv_ref, wlat_ref, wg_ref, qn_ref, wuq_ref, kvn_ref, wuk_ref, wuv_ref,
                   cosq_ref, sinq_ref, cosk_ref, sink_ref, vone_ref,
                   qk_ref, vnat_ref, qm_ref, km_ref, vt_ref, g_ref):
    bf = jnp.bfloat16
    f32 = jnp.float32
    hn = _rms(x_ref[...], nmix_ref[...]).astype(bf)

    lat = jnp.dot(hn, wlat_ref[...], preferred_element_type=f32)
    cq = _rms(lat[:, :MLA_Q_RANK], qn_ref[...]).astype(bf)
    ckv = _rms(lat[:, MLA_Q_RANK:MLA_Q_RANK + MLA_KV_RANK], kvn_ref[...]).astype(bf)
    kr = lat[:, MLA_Q_RANK + MLA_KV_RANK:]
    kr = _rope_group(kr, cosk_ref[...], sink_ref[...])

    qkv = jnp.dot(hn, wqkv_ref[...], preferred_element_type=f32)
    qk_ref[:, :NA_W] = (qkv[:, :NA_W] * (NA_HEAD_DIM ** -0.5 * LOG2E)).astype(bf)
    qk_ref[:, NA_W:] = qkv[:, NA_W:2 * NA_W].astype(bf)
    vnat_ref[...] = qkv[:, 2 * NA_W:].T.astype(bf)

    q = jnp.dot(cq, wuq_ref[...], preferred_element_type=f32)
    kn = jnp.dot(ckv, wuk_ref[...], preferred_element_type=f32)
    vv = jnp.dot(ckv, wuv_ref[...], preferred_element_type=f32)

    g_ref[...] = jnp.dot(hn, wg_ref[...], preferred_element_type=f32).astype(bf)

    cosq = cosq_ref[...]
    sinq = sinq_ref[...]
    for h in range(MLA_HEADS):
        sl = slice(h * HEAD_GROUP, (h + 1) * HEAD_GROUP)
        qm_ref[:, sl] = _rope_group(q[:, sl], cosq, sinq).astype(bf)
        km_ref[:, sl] = (kn[:, sl] + kr).astype(bf)
    vt_ref[...] = (vv + vone_ref[...]).T.astype(bf)


def _inproj(x2, nmix, wqkv, wlat, wg, qn, wuq, kvn, wuk, wuv, cosq, sinq, cosk, sink, vone, tm):
    t = x2.shape[0]
    n_tab = cosq.shape[0] // tm
    row = lambda i: (i, 0)
    col = lambda i: (0, i)
    tab = lambda i: (i % n_tab, 0)
    bf = jnp.bfloat16
    return pl.pallas_call(
        _inproj_kernel,
        grid=(t // tm,),
        in_specs=[
            pl.BlockSpec((tm, D_MODEL), row),
            _const_spec((1, D_MODEL)),
            _const_spec((D_MODEL, C_CQ)),
            _const_spec((D_MODEL, C_G - C_CQ)),
            _const_spec((D_MODEL, 2 * D_MODEL)),
            _const_spec((1, MLA_Q_RANK)),
            _const_spec((MLA_Q_RANK, MLA_QK)),
            _const_spec((1, MLA_KV_RANK)),
            _const_spec((MLA_KV_RANK, MLA_QK)),
            _const_spec((MLA_KV_RANK, MLA_QK)),
            pl.BlockSpec((tm, HEAD_GROUP), tab),
            pl.BlockSpec((tm, HEAD_GROUP), tab),
            pl.BlockSpec((tm, HEAD_GROUP), tab),
            pl.BlockSpec((tm, HEAD_GROUP), tab),
            _const_spec((1, MLA_QK)),
        ],
        out_specs=[
            pl.BlockSpec((tm, 2 * NA_W), row),
            pl.BlockSpec((NA_W, tm), col),
            pl.BlockSpec((tm, MLA_QK), row),
            pl.BlockSpec((tm, MLA_QK), row),
            pl.BlockSpec((MLA_QK, tm), col),
            pl.BlockSpec((tm, 2 * D_MODEL), row),
        ],
        out_shape=[
            jax.ShapeDtypeStruct((t, 2 * NA_W), bf),
            jax.ShapeDtypeStruct((NA_W, t), bf),
            jax.ShapeDtypeStruct((t, MLA_QK), bf),
            jax.ShapeDtypeStruct((t, MLA_QK), bf),
            jax.ShapeDtypeStruct((MLA_QK, t), bf),
            jax.ShapeDtypeStruct((t, 2 * D_MODEL), bf),
        ],
        compiler_params=pltpu.CompilerParams(
            dimension_semantics=("parallel",), vmem_limit_bytes=VMEM_LIMIT),
        name="inproj",
    )(x2, nmix, wqkv, wlat, wg, qn, wuq, kvn, wuk, wuv, cosq, sinq, cosk, sink, vone)


def _na_kernel(q_ref, k_ref, vt_ref, km_ref, vtm_ref, tbl_ref, o_ref, *, n_blocks):
    bf = jnp.bfloat16
    f32 = jnp.float32
    nb = q_ref.shape[0] // NA_Q
    rows = n_blocks * NA_ROWS_PER_BLOCK
    lane = lax.broadcasted_iota(jnp.int32, (NA_Q, LANES), 1)
    in_head = (lane < NA_HEAD_DIM, lane >= NA_HEAD_DIM)

    def window(i):
        j = pl.program_id(2) * nb + i
        ws = jnp.clip(j * NA_ROWS_PER_BLOCK - NA_WIN_H // 2, 0, rows - NA_KEY_ROWS)
        variant = jnp.where(j == 0, 0, jnp.where(j == n_blocks - 1, 2, 1))
        return pl.multiple_of(ws * GRID_W, NA_ROWS_PER_BLOCK * GRID_W), variant

    wins = [window(i) for i in range(nb)]
    s_val, p_val = {}, {}

    def scores(u):
        start, variant = wins[u]
        q2 = q_ref[u * NA_Q:(u + 1) * NA_Q, :]
        zero = jnp.zeros_like(q2)
        qq = jnp.concatenate([jnp.where(in_head[0], q2, zero), jnp.where(in_head[1], q2, zero)], axis=0)
        s_g = lax.dot_general(k_ref[pl.ds(start, NA_K), :], qq, _NT, preferred_element_type=f32)
        s_g = s_g + tbl_ref[variant, 0, :NA_K, :]
        s_m = lax.dot_general(km_ref[...], qq, _NT, preferred_element_type=f32)
        s_m = s_m + tbl_ref[variant, 0, NA_K:, :]
        s_val[u] = (s_g, s_m)
        mx = jnp.maximum(_fold_rows(s_g, jnp.max), _fold_rows(s_m, jnp.max))
        return jnp.max(mx, axis=0, keepdims=True)

    def probs(u, m):
        s_g, s_m = s_val.pop(u)
        p_g = jnp.exp2(s_g - m)
        p_m = jnp.exp2(s_m - m)
        p_val[u] = (p_g.astype(bf), p_m.astype(bf))
        return jnp.sum(_fold_rows(p_g, jnp.sum) + _fold_rows(p_m, jnp.sum), axis=0, keepdims=True)

    def weighted(u):
        start = wins[u][0]
        p_g, p_m = p_val.pop(u)
        return (jnp.dot(vt_ref[:, pl.ds(start, NA_K)], p_g, preferred_element_type=f32)
                + jnp.dot(vtm_ref[...], p_m, preferred_element_type=f32))

    col_max, denom, out = {}, {}, {}
    for k in range(nb + 2):
        if k < nb:
            col_max[k] = scores(k)
        if 1 <= k <= nb:
            denom[k - 1] = probs(k - 1, col_max[k - 1])
        if k >= 2:
            out[k - 2] = weighted(k - 2) / denom[k - 2]
    for i in range(nb):
        o_t = jnp.concatenate([out[i][:NA_HEAD_DIM, :NA_Q], out[i][NA_HEAD_DIM:, NA_Q:]], axis=0)
        o_ref[i * NA_Q:(i + 1) * NA_Q, :] = o_t.T.astype(bf)


def _na_attention(qk, vt, k_meta, vt_meta, tbl, batch, seq):
    n_blocks = seq // NA_Q
    n_steps = n_blocks // NA_BLOCKS_PER_STEP
    n_pairs = NA_HEADS // 2
    kcol = NA_W // LANES
    tq = NA_BLOCKS_PER_STEP * NA_Q
    return pl.pallas_call(
        functools.partial(_na_kernel, n_blocks=n_blocks),
        grid=(n_pairs, batch, n_steps),
        in_specs=[
            pl.BlockSpec((tq, LANES), lambda hp, b, j: (b * n_steps + j, hp)),
            pl.BlockSpec((seq, LANES), lambda hp, b, j: (b, kcol + hp)),
            pl.BlockSpec((LANES, seq), lambda hp, b, j: (hp, b)),
            pl.BlockSpec((N_META, LANES), lambda hp, b, j: (0, kcol + hp)),
            pl.BlockSpec((LANES, N_META), lambda hp, b, j: (hp, 0)),
            pl.BlockSpec((3, 1, NA_KT, 2 * NA_Q), lambda hp, b, j: (0, hp, 0, 0)),
        ],
        out_specs=pl.BlockSpec((tq, LANES), lambda hp, b, j: (b * n_steps + j, hp)),
        out_shape=jax.ShapeDtypeStruct((batch * seq, NA_W), jnp.bfloat16),
        compiler_params=pltpu.CompilerParams(
            dimension_semantics=("parallel", "parallel", "arbitrary"), vmem_limit_bytes=VMEM_LIMIT),
        name="na_attn",
    )(qk, qk, vt, k_meta, vt_meta, tbl)


def _na_bias_kernel(w_ref, o_ref, *, rows, n_blocks):
    kh = min(NA_WIN_H, rows)
    shape = (GRID_W, LANES)
    lane = lax.broadcasted_iota(jnp.int32, shape, 1)
    kc = lax.broadcasted_iota(jnp.int32, shape, 0)
    qc = lane % GRID_W
    cstart = jnp.clip(qc - NA_WIN_W // 2, 0, GRID_W - NA_WIN_W)
    cvalid = (kc >= cstart) & (kc < cstart + NA_WIN_W)
    neg = jnp.full(shape, NEG, jnp.float32)
    halves = {}

    def toeplitz(hh, dr, half):
        if (hh, dr, half) not in halves:
            row = jnp.broadcast_to(w_ref[hh, dr:dr + 1, :], shape)
            halves[hh, dr, half] = pltpu.roll(row, half * GRID_W, 1, stride=1, stride_axis=0)
        return halves[hh, dr, half]

    for variant, j in enumerate((0, 1, n_blocks - 1)):
        r0 = j * NA_ROWS_PER_BLOCK
        ws = min(max(r0 - NA_WIN_H // 2, 0), rows - NA_KEY_ROWS)
        o_ref[variant, 0, NA_K:, :] = jnp.zeros((N_META, 2 * NA_Q), jnp.float32)
        for hh in range(2):
            for rq_pair in range(NA_ROWS_PER_BLOCK // 2):
                for rk in range(NA_KEY_ROWS):
                    krow = ws + rk
                    tiles = []
                    for half in range(2):
                        r = r0 + 2 * rq_pair + half
                        rs = min(max(r - kh // 2, 0), rows - kh)
                        ok = rs <= krow < rs + kh
                        tiles.append(toeplitz(hh, krow - r + NA_WIN_H - 1, half) if ok else neg)
                    tile = jnp.where(cvalid, jnp.where(lane < GRID_W, tiles[0], tiles[1]), NEG)
                    lane0 = hh * NA_Q + rq_pair * LANES
                    o_ref[variant, 0, rk * GRID_W:(rk + 1) * GRID_W, lane0:lane0 + LANES] = tile


def _na_bias_table(rpb, seq):
    rows = seq // GRID_W
    n_blocks = rows // NA_ROWS_PER_BLOCK
    n_dr = 2 * NA_WIN_H - 1
    n_pairs = NA_HEADS // 2
    rpb = rpb.astype(jnp.float32) * LOG2E
    side = GRID_W - NA_WIN_W
    rpb_pad = jnp.pad(rpb, ((0, 0), (0, 0), (side, side)))
    w = jnp.concatenate([rpb_pad[..., GRID_W - 1:], jnp.zeros(rpb.shape[:2] + (1,), jnp.float32),
                         rpb_pad[..., :GRID_W - 1]], axis=-1)
    w = jnp.roll(jnp.flip(w, axis=-1), 1, axis=-1)
    return pl.pallas_call(
        functools.partial(_na_bias_kernel, rows=rows, n_blocks=n_blocks),
        grid=(n_pairs,),
        in_specs=[pl.BlockSpec((2, n_dr, LANES), lambda hp: (hp, 0, 0))],
        out_specs=pl.BlockSpec((3, 1, NA_KT, 2 * NA_Q), lambda hp: (0, hp, 0, 0)),
        out_shape=jax.ShapeDtypeStruct((3, n_pairs, NA_KT, 2 * NA_Q), jnp.float32),
        compiler_params=pltpu.CompilerParams(dimension_semantics=("parallel",), vmem_limit_bytes=VMEM_LIMIT),
        name="na_bias",
    )(w)


def _mla_kernel(q_ref, k_ref, vt_ref, km_ref, vtm_ref, o_ref, s_ref, p_ref):
    bf = jnp.bfloat16
    f32 = jnp.float32
    n_sub = q_ref.shape[0] // MLA_SUB
    n_chunks = k_ref.shape[0] // MLA_TK
    head = (slice(0, HEAD_GROUP), slice(HEAD_GROUP, 2 * HEAD_GROUP))
    chunk = [slice(c * MLA_TK, (c + 1) * MLA_TK) for c in range(n_chunks)]
    units = [(slice(i * MLA_SUB, (i + 1) * MLA_SUB), hh) for i in range(n_sub) for hh in range(2)]
    n_units = len(units)

    def meta_scores(u):
        rows, hh = units[u]
        return lax.dot_general(km_ref[:, head[hh]], q_ref[rows, head[hh]], _NT, preferred_element_type=f32)

    def scores(u, c):
        rows, hh = units[u]
        s = lax.dot_general(k_ref[chunk[c], head[hh]], q_ref[rows, head[hh]], _NT, preferred_element_type=f32)
        s_ref[u % 2, chunk[c], :] = s
        return _fold_rows(s, jnp.max)

    def probs(u, c, m):
        p_ref[u % 2, chunk[c], :] = jnp.exp2(s_ref[u % 2, chunk[c], :] - m).astype(bf)

    def weighted(u, c):
        hh = units[u][1]
        return jnp.dot(vt_ref[head[hh], chunk[c]], p_ref[u % 2, chunk[c], :], preferred_element_type=f32)

    s_meta, col_max, acc = {}, {}, {}
    for k in range(n_units + 2):
        if k < n_units:
            s_meta[k] = meta_scores(k)
            mx = _fold_rows(s_meta[k], jnp.max)
        if k >= 2:
            p_meta = jnp.exp2(s_meta[k - 2] - col_max[k - 2]).astype(bf)
            acc[k - 2] = jnp.dot(vtm_ref[head[units[k - 2][1]], :], p_meta, preferred_element_type=f32)
        for c in range(n_chunks):
            if k < n_units:
                mx = jnp.maximum(mx, scores(k, c))
            if 1 <= k <= n_units:
                probs(k - 1, c, col_max[k - 1])
            if k >= 2:
                acc[k - 2] = acc[k - 2] + weighted(k - 2, c)
        if k < n_units:
            col_max[k] = jnp.max(mx, axis=0, keepdims=True)

    for i in range(n_sub):
        o0, o1 = [(acc[2 * i + hh] / acc[2 * i + hh][MLA_V_DIM:MLA_V_DIM + 1, :])[:MLA_V_DIM] for hh in range(2)]
        o_ref[units[2 * i][0], :] = jnp.concatenate([o0, o1], axis=0).T.astype(bf)


def _mla_attention(qm, km, vt, km_meta, vt_meta, batch, seq, tq):
    nq = seq // tq
    n_pairs = MLA_HEADS // 2
    pair = 2 * HEAD_GROUP
    return pl.pallas_call(
        _mla_kernel,
        grid=(batch, n_pairs, nq),
        in_specs=[
            pl.BlockSpec((tq, pair), lambda b, hp, i: (b * nq + i, hp)),
            pl.BlockSpec((seq, pair), lambda b, hp, i: (b, hp)),
            pl.BlockSpec((pair, seq), lambda b, hp, i: (hp, b)),
            pl.BlockSpec((N_META, pair), lambda b, hp, i: (0, hp)),
            pl.BlockSpec((pair, N_META), lambda b, hp, i: (hp, 0)),
        ],
        out_specs=pl.BlockSpec((tq, 2 * MLA_V_DIM), lambda b, hp, i: (b * nq + i, hp)),
        out_shape=jax.ShapeDtypeStruct((batch * seq, MLA_HEADS * MLA_V_DIM), jnp.bfloat16),
        scratch_shapes=[pltpu.VMEM((2, seq, MLA_SUB), jnp.float32), pltpu.VMEM((2, seq, MLA_SUB), jnp.bfloat16)],
        compiler_params=pltpu.CompilerParams(
            dimension_semantics=("parallel", "parallel", "arbitrary"), vmem_limit_bytes=VMEM_LIMIT),
        name="mla_attn",
    )(qm, km, vt, km_meta, vt_meta)


def _post_kernel(x_ref, ona_ref, omla_ref, g_ref, wna_ref, wmla_ref, wout_ref, nffn_ref,
                 wff1_ref, wff2_ref, nfin_ref, o_ref):
    bf = jnp.bfloat16
    f32 = jnp.float32
    a = jnp.dot(ona_ref[...], wna_ref[...], preferred_element_type=f32)
    b = jnp.dot(omla_ref[...], wmla_ref[...], preferred_element_type=f32)
    g_na = g_ref[:, :D_MODEL].astype(f32)
    g_mla = g_ref[:, D_MODEL:].astype(f32)
    merged = jax.nn.sigmoid(g_na) * a + jax.nn.sigmoid(g_mla) * b
    h = x_ref[...] + jnp.dot(merged.astype(bf), wout_ref[...], preferred_element_type=f32)
    fn = _rms(h, nffn_ref[...]).astype(bf)
    u = jnp.dot(fn, wff1_ref[...], preferred_element_type=f32)
    u = jnp.square(jnp.maximum(u, 0.0)).astype(bf)
    h = h + jnp.dot(u, wff2_ref[...], preferred_element_type=f32)
    o_ref[...] = _rms(h, nfin_ref[...])


def _post(x2, o_na, o_mla, g, wna, wmla, wout, nffn, wff1, wff2, nfin, tm):
    t = x2.shape[0]
    row = lambda i: (i, 0)
    return pl.pallas_call(
        _post_kernel,
        grid=(t // tm,),
        in_specs=[
            pl.BlockSpec((tm, D_MODEL), row),
            pl.BlockSpec((tm, NA_W), row),
            pl.BlockSpec((tm, MLA_HEADS * MLA_V_DIM), row),
            pl.BlockSpec((tm, 2 * D_MODEL), row),
            _const_spec((NA_W, D_MODEL)),
            _const_spec((MLA_HEADS * MLA_V_DIM, D_MODEL)),
            _const_spec((D_MODEL, D_MODEL)),
            _const_spec((1, D_MODEL)),
            _const_spec((D_MODEL, D_FF)),
            _const_spec((D_FF, D_MODEL)),
            _const_spec((1, D_MODEL)),
        ],
        out_specs=pl.BlockSpec((tm, D_MODEL), row),
        out_shape=jax.ShapeDtypeStruct((t, D_MODEL), jnp.float32),
        compiler_params=pltpu.CompilerParams(
            dimension_semantics=("parallel",), vmem_limit_bytes=VMEM_LIMIT),
        name="post",
    )(x2, o_na, o_mla, g, wna, wmla, wout, nffn, wff1, wff2, nfin)


def _head_groups(w, per_head, lane0):
    k = w.shape[0]
    w = w.reshape(k, MLA_HEADS, per_head)
    w = jnp.pad(w, ((0, 0), (0, 0), (lane0, HEAD_GROUP - lane0 - per_head)))
    return w.reshape(k, MLA_HEADS * HEAD_GROUP)


def _rope_tables(pos0, n, scale):
    pos = np.arange(pos0, pos0 + n, dtype=np.float64)
    inv_freq = 1.0 / (ROPE_THETA ** (np.arange(0, MLA_ROPE_DIM, 2, dtype=np.float64) / MLA_ROPE_DIM))
    ang = pos[:, None] * inv_freq[None, :]
    cos, sin = np.cos(ang), np.sin(ang)
    ones = np.ones((n, ROPE_LANE0))
    tail = HEAD_GROUP - ROPE_LANE0 - MLA_ROPE_DIM
    cos_t = np.concatenate([ones, cos, cos, np.ones((n, tail))], axis=1) * scale
    sin_t = np.concatenate([0 * ones, -sin, sin, np.zeros((n, tail))], axis=1) * scale
    return jnp.asarray(cos_t, jnp.float32), jnp.asarray(sin_t, jnp.float32)


def kernel(x, meta, norm_mix, w_in, na_rpb, mla_q_norm, w_uq, mla_kv_norm, w_ukv, w_na_out, w_mla_out,
           w_out, norm_ffn, w_ff1, w_ff2, norm_final):
    assert norm_mix.shape[0] == 1, "single-layer block"
    batch, seq, d = x.shape
    bf = jnp.bfloat16
    f32 = jnp.float32
    x2 = x.reshape(batch * seq, d)

    wi = w_in[0]
    kr_cols = jnp.pad(wi[:, C_KR:C_KR + MLA_ROPE_DIM],
                      ((0, 0), (ROPE_LANE0, HEAD_GROUP - ROPE_LANE0 - MLA_ROPE_DIM)))
    wqkv = wi[:, :C_CQ].astype(bf)
    wlat = jnp.concatenate([wi[:, C_CQ:C_KR], kr_cols], axis=1).astype(bf)
    wg = wi[:, C_KR + MLA_ROPE_DIM:].astype(bf)
    wuq = _head_groups(w_uq[0], MLA_NOPE_DIM + MLA_ROPE_DIM, 0).astype(bf)
    wukv = w_ukv[0].reshape(MLA_KV_RANK, MLA_HEADS, MLA_NOPE_DIM + MLA_V_DIM)
    wuk = _head_groups(wukv[:, :, :MLA_NOPE_DIM].reshape(MLA_KV_RANK, -1), MLA_NOPE_DIM, 0).astype(bf)
    wuv = _head_groups(wukv[:, :, MLA_NOPE_DIM:].reshape(MLA_KV_RANK, -1), MLA_V_DIM, 0).astype(bf)
    vone = (jnp.arange(MLA_QK) % HEAD_GROUP == MLA_V_DIM).astype(f32)[None]

    mla_scale = (MLA_NOPE_DIM + MLA_ROPE_DIM) ** -0.5 * LOG2E
    cosq, sinq = _rope_tables(N_META, seq, mla_scale)
    cosk, sink = _rope_tables(N_META, seq, 1.0)
    cosq_m, sinq_m = _rope_tables(0, META_PAD, mla_scale)
    cosk_m, sink_m = _rope_tables(0, META_PAD, 1.0)

    small = (norm_mix[0][None], wqkv, wlat, wg, mla_q_norm[0][None], wuq, mla_kv_norm[0][None], wuk, wuv)
    qk, vnat, qm, km, vt, g = _inproj(x2, *small, cosq, sinq, cosk, sink, vone, tm=512)
    meta_pad = jnp.pad(meta.astype(f32), ((0, META_PAD - N_META), (0, 0)))
    qk_meta, vnat_meta, _, km_meta, vt_meta, _ = _inproj(meta_pad, *small, cosq_m, sinq_m, cosk_m, sink_m, vone,
                                                         tm=META_PAD)

    tbl = _na_bias_table(na_rpb[0], seq)
    o_na = _na_attention(qk, vnat, qk_meta[:N_META], vnat_meta[:, :N_META], tbl, batch, seq)
    o_mla = _mla_attention(qm, km, vt, km_meta[:N_META], vt_meta[:, :N_META], batch, seq, tq=4096)

    out = _post(x2, o_na, o_mla, g, w_na_out[0].astype(bf), w_mla_out[0].astype(bf), w_out[0].astype(bf),
                norm_ffn[0][None], w_ff1[0].astype(bf), w_ff2[0].astype(bf), norm_final[None], tm=512)
    return out.reshape(batch, seq, d)
```

```python
import functools

import numpy as np
import jax
import jax.numpy as jnp
from jax import lax
from jax.experimental import pallas as pl
from jax.experimental.pallas import tpu as pltpu

D_MODEL = 1024
N_META = 16
GRID_W = 64
NA_HEADS = 8
NA_HEAD_DIM = 64
NA_WIN_H = 8
NA_WIN_W = 16
NA_W = NA_HEADS * NA_HEAD_DIM
MLA_HEADS = 8
MLA_NOPE_DIM = 64
MLA_ROPE_DIM = 32
MLA_V_DIM = 64
MLA_Q_RANK = 384
MLA_KV_RANK = 256
ROPE_THETA = 10000.0
D_FF = 4 * D_MODEL
EPS = 1e-6

LANES = 128
SUBLANES = 8
HEAD_GROUP = 128
ROPE_LANE0 = MLA_NOPE_DIM
MLA_QK = MLA_HEADS * HEAD_GROUP
NEG = -1e30

C_CQ = 3 * NA_W
C_CKV = C_CQ + MLA_Q_RANK
C_KR = C_CKV + MLA_KV_RANK
C_G = C_KR + HEAD_GROUP

NA_ROWS_PER_BLOCK = 4
NA_KEY_ROWS = NA_ROWS_PER_BLOCK + NA_WIN_H
NA_Q = NA_ROWS_PER_BLOCK * GRID_W
NA_K = NA_KEY_ROWS * GRID_W
NA_KT = NA_K + N_META
NA_BLOCKS_PER_STEP = 16
META_PAD = LANES
LOG2E = float(np.log2(np.e))
MLA_TK = 256
MLA_SUB = 512

VMEM_LIMIT = 56 * 1024 * 1024

_NT = (((1,), (1,)), ((), ()))


def _const_spec(shape):
    nd = len(shape)
    return pl.BlockSpec(shape, lambda *_: (0,) * nd, pipeline_mode=pl.Buffered(1))


def _rms(x, g):
    return x * lax.rsqrt(jnp.mean(x * x, axis=-1, keepdims=True) + EPS) * g


def _rope_group(x, cos, sin):
    lane = lax.broadcasted_iota(jnp.int32, x.shape, 1)
    partner = jnp.where(lane < ROPE_LANE0 + MLA_ROPE_DIM // 2,
                        pltpu.roll(x, HEAD_GROUP - MLA_ROPE_DIM // 2, 1),
                        pltpu.roll(x, MLA_ROPE_DIM // 2, 1))
    return x * cos + partner * sin


def _fold_rows(x, op):
    return op(x.reshape(x.shape[0] // SUBLANES, SUBLANES, x.shape[1]), axis=0)


def _inproj_kernel(x_ref, nmix_ref, wqkv_ref, wlat_ref, wg_ref, qn_ref, wuq_ref, kvn_ref, wuk_ref, wuv_ref,
                   cosq_ref, sinq_ref, cosk_ref, sink_ref, vone_ref,
                   qk_ref, vnat_ref, qm_ref, km_ref, vt_ref, g_ref):
    bf = jnp.bfloat16
    f32 = jnp.float32
    hn = _rms(x_ref[...], nmix_ref[...]).astype(bf)

    lat = jnp.dot(hn, wlat_ref[...], preferred_element_type=f32)
    cq = _rms(lat[:, :MLA_Q_RANK], qn_ref[...]).astype(bf)
    ckv = _rms(lat[:, MLA_Q_RANK:MLA_Q_RANK + MLA_KV_RANK], kvn_ref[...]).astype(bf)
    kr = lat[:, MLA_Q_RANK + MLA_KV_RANK:]
    kr = _rope_group(kr, cosk_ref[...], sink_ref[...])

    qkv = jnp.dot(hn, wqkv_ref[...], preferred_element_type=f32)
    qk_ref[:, :NA_W] = (qkv[:, :NA_W] * (NA_HEAD_DIM ** -0.5 * LOG2E)).astype(bf)
    qk_ref[:, NA_W:] = qkv[:, NA_W:2 * NA_W].astype(bf)
    vnat_ref[...] = qkv[:, 2 * NA_W:].T.astype(bf)

    q = jnp.dot(cq, wuq_ref[...], preferred_element_type=f32)
    kn = jnp.dot(ckv, wuk_ref[...], preferred_element_type=f32)
    vv = jnp.dot(ckv, wuv_ref[...], preferred_element_type=f32)

    g_ref[...] = jnp.dot(hn, wg_ref[...], preferred_element_type=f32).astype(bf)

    cosq = cosq_ref[...]
    sinq = sinq_ref[...]
    for h in range(MLA_HEADS):
        sl = slice(h * HEAD_GROUP, (h + 1) * HEAD_GROUP)
        qm_ref[:, sl] = _rope_group(q[:, sl], cosq, sinq).astype(bf)
        km_ref[:, sl] = (kn[:, sl] + kr).astype(bf)
    vt_ref[...] = (vv + vone_ref[...]).T.astype(bf)


def _inproj(x2, nmix, wqkv, wlat, wg, qn, wuq, kvn, wuk, wuv, cosq, sinq, cosk, sink, vone, tm):
    t = x2.shape[0]
    n_tab = cosq.shape[0] // tm
    row = lambda i: (i, 0)
    col = lambda i: (0, i)
    tab = lambda i: (i % n_tab, 0)
    bf = jnp.bfloat16
    return pl.pallas_call(
        _inproj_kernel,
        grid=(t // tm,),
        in_specs=[
            pl.BlockSpec((tm, D_MODEL), row),
            _const_spec((1, D_MODEL)),
            _const_spec((D_MODEL, C_CQ)),
            _const_spec((D_MODEL, C_G - C_CQ)),
            _const_spec((D_MODEL, 2 * D_MODEL)),
            _const_spec((1, MLA_Q_RANK)),
            _const_spec((MLA_Q_RANK, MLA_QK)),
            _const_spec((1, MLA_KV_RANK)),
            _const_spec((MLA_KV_RANK, MLA_QK)),
            _const_spec((MLA_KV_RANK, MLA_QK)),
            pl.BlockSpec((tm, HEAD_GROUP), tab),
            pl.BlockSpec((tm, HEAD_GROUP), tab),
            pl.BlockSpec((tm, HEAD_GROUP), tab),
            pl.BlockSpec((tm, HEAD_GROUP), tab),
            _const_spec((1, MLA_QK)),
        ],
        out_specs=[
            pl.BlockSpec((tm, 2 * NA_W), row),
            pl.BlockSpec((NA_W, tm), col),
            pl.BlockSpec((tm, MLA_QK), row),
            pl.BlockSpec((tm, MLA_QK), row),
            pl.BlockSpec((MLA_QK, tm), col),
            pl.BlockSpec((tm, 2 * D_MODEL), row),
        ],
        out_shape=[
            jax.ShapeDtypeStruct((t, 2 * NA_W), bf),
            jax.ShapeDtypeStruct((NA_W, t), bf),
            jax.ShapeDtypeStruct((t, MLA_QK), bf),
            jax.ShapeDtypeStruct((t, MLA_QK), bf),
            jax.ShapeDtypeStruct((MLA_QK, t), bf),
            jax.ShapeDtypeStruct((t, 2 * D_MODEL), bf),
        ],
        compiler_params=pltpu.CompilerParams(
            dimension_semantics=("parallel",), vmem_limit_bytes=VMEM_LIMIT),
        name="inproj",
    )(x2, nmix, wqkv, wlat, wg, qn, wuq, kvn, wuk, wuv, cosq, sinq, cosk, sink, vone)


def _na_kernel(q_ref, k_ref, vt_ref, km_ref, vtm_ref, tbl_ref, o_ref, *, n_blocks):
    bf = jnp.bfloat16
    f32 = jnp.float32
    nb = q_ref.shape[0] // NA_Q
    rows = n_blocks * NA_ROWS_PER_BLOCK
    lane = lax.broadcasted_iota(jnp.int32, (NA_Q, LANES), 1)
    in_head = (lane < NA_HEAD_DIM, lane >= NA_HEAD_DIM)

    def window(i):
        j = pl.program_id(2) * nb + i
        ws = jnp.clip(j * NA_ROWS_PER_BLOCK - NA_WIN_H // 2, 0, rows - NA_KEY_ROWS)
        variant = jnp.where(j == 0, 0, jnp.where(j == n_blocks - 1, 2, 1))
        return pl.multiple_of(ws * GRID_W, NA_ROWS_PER_BLOCK * GRID_W), variant

    wins = [window(i) for i in range(nb)]
    s_val, p_val = {}, {}

    def scores(u):
        start, variant = wins[u]
        q2 = q_ref[u * NA_Q:(u + 1) * NA_Q, :]
        zero = jnp.zeros_like(q2)
        qq = jnp.concatenate([jnp.where(in_head[0], q2, zero), jnp.where(in_head[1], q2, zero)], axis=0)
        s_g = lax.dot_general(k_ref[pl.ds(start, NA_K), :], qq, _NT, preferred_element_type=f32)
        s_g = s_g + tbl_ref[variant, 0, :NA_K, :]
        s_m = lax.dot_general(km_ref[...], qq, _NT, preferred_element_type=f32)
        s_m = s_m + tbl_ref[variant, 0, NA_K:, :]
        s_val[u] = (s_g, s_m)
        mx = jnp.maximum(_fold_rows(s_g, jnp.max), _fold_rows(s_m, jnp.max))
        return jnp.max(mx, axis=0, keepdims=True)

    def probs(u, m):
        s_g, s_m = s_val.pop(u)
        p_g = jnp.exp2(s_g - m)
        p_m = jnp.exp2(s_m - m)
        p_val[u] = (p_g.astype(bf), p_m.astype(bf))
        return jnp.sum(_fold_rows(p_g, jnp.sum) + _fold_rows(p_m, jnp.sum), axis=0, keepdims=True)

    def weighted(u):
        start = wins[u][0]
        p_g, p_m = p_val.pop(u)
        return (jnp.dot(vt_ref[:, pl.ds(start, NA_K)], p_g, preferred_element_type=f32)
                + jnp.dot(vtm_ref[...], p_m, preferred_element_type=f32))

    col_max, denom, out = {}, {}, {}
    for k in range(nb + 2):
        if k < nb:
            col_max[k] = scores(k)
        if 1 <= k <= nb:
            denom[k - 1] = probs(k - 1, col_max[k - 1])
        if k >= 2:
            out[k - 2] = weighted(k - 2) / denom[k - 2]
    for i in range(nb):
        o_t = jnp.concatenate([out[i][:NA_HEAD_DIM, :NA_Q], out[i][NA_HEAD_DIM:, NA_Q:]], axis=0)
        o_ref[i * NA_Q:(i + 1) * NA_Q, :] = o_t.T.astype(bf)


def _na_attention(qk, vt, k_meta, vt_meta, tbl, batch, seq):
    n_blocks = seq // NA_Q
    n_steps = n_blocks // NA_BLOCKS_PER_STEP
    n_pairs = NA_HEADS // 2
    kcol = NA_W // LANES
    tq = NA_BLOCKS_PER_STEP * NA_Q
    return pl.pallas_call(
        functools.partial(_na_kernel, n_blocks=n_blocks),
        grid=(n_pairs, batch, n_steps),
        in_specs=[
            pl.BlockSpec((tq, LANES), lambda hp, b, j: (b * n_steps + j, hp)),
            pl.BlockSpec((seq, LANES), lambda hp, b, j: (b, kcol + hp)),
            pl.BlockSpec((LANES, seq), lambda hp, b, j: (hp, b)),
            pl.BlockSpec((N_META, LANES), lambda hp, b, j: (0, kcol + hp)),
            pl.BlockSpec((LANES, N_META), lambda hp, b, j: (hp, 0)),
            pl.BlockSpec((3, 1, NA_KT, 2 * NA_Q), lambda hp, b, j: (0, hp, 0, 0)),
        ],
        out_specs=pl.BlockSpec((tq, LANES), lambda hp, b, j: (b * n_steps + j, hp)),
        out_shape=jax.ShapeDtypeStruct((batch * seq, NA_W), jnp.bfloat16),
        compiler_params=pltpu.CompilerParams(
            dimension_semantics=("parallel", "parallel", "arbitrary"), vmem_limit_bytes=VMEM_LIMIT),
        name="na_attn",
    )(qk, qk, vt, k_meta, vt_meta, tbl)


def _na_bias_kernel(w_ref, o_ref, *, rows, n_blocks):
    kh = min(NA_WIN_H, rows)
    shape = (GRID_W, LANES)
    lane = lax.broadcasted_iota(jnp.int32, shape, 1)
    kc = lax.broadcasted_iota(jnp.int32, shape, 0)
    qc = lane % GRID_W
    cstart = jnp.clip(qc - NA_WIN_W // 2, 0, GRID_W - NA_WIN_W)
    cvalid = (kc >= cstart) & (kc < cstart + NA_WIN_W)
    neg = jnp.full(shape, NEG, jnp.float32)
    halves = {}

    def toeplitz(hh, dr, half):
        if (hh, dr, half) not in halves:
            row = jnp.broadcast_to(w_ref[hh, dr:dr + 1, :], shape)
            halves[hh, dr, half] = pltpu.roll(row, half * GRID_W, 1, stride=1, stride_axis=0)
        return halves[hh, dr, half]

    for variant, j in enumerate((0, 1, n_blocks - 1)):
        r0 = j * NA_ROWS_PER_BLOCK
        ws = min(max(r0 - NA_WIN_H // 2, 0), rows - NA_KEY_ROWS)
        o_ref[variant, 0, NA_K:, :] = jnp.zeros((N_META, 2 * NA_Q), jnp.float32)
        for hh in range(2):
            for rq_pair in range(NA_ROWS_PER_BLOCK // 2):
                for rk in range(NA_KEY_ROWS):
                    krow = ws + rk
                    tiles = []
                    for half in range(2):
                        r = r0 + 2 * rq_pair + half
                        rs = min(max(r - kh // 2, 0), rows - kh)
                        ok = rs <= krow < rs + kh
                        tiles.append(toeplitz(hh, krow - r + NA_WIN_H - 1, half) if ok else neg)
                    tile = jnp.where(cvalid, jnp.where(lane < GRID_W, tiles[0], tiles[1]), NEG)
                    lane0 = hh * NA_Q + rq_pair * LANES
                    o_ref[variant, 0, rk * GRID_W:(rk + 1) * GRID_W, lane0:lane0 + LANES] = tile


def _na_bias_table(rpb, seq):
    rows = seq // GRID_W
    n_blocks = rows // NA_ROWS_PER_BLOCK
    n_dr = 2 * NA_WIN_H - 1
    n_pairs = NA_HEADS // 2
    rpb = rpb.astype(jnp.float32) * LOG2E
    side = GRID_W - NA_WIN_W
    rpb_pad = jnp.pad(rpb, ((0, 0), (0, 0), (side, side)))
    w = jnp.concatenate([rpb_pad[..., GRID_W - 1:], jnp.zeros(rpb.shape[:2] + (1,), jnp.float32),
                         rpb_pad[..., :GRID_W - 1]], axis=-1)
    w = jnp.roll(jnp.flip(w, axis=-1), 1, axis=-1)
    return pl.pallas_call(
        functools.partial(_na_bias_kernel, rows=rows, n_blocks=n_blocks),
        grid=(n_pairs,),
        in_specs=[pl.BlockSpec((2, n_dr, LANES), lambda hp: (hp, 0, 0))],
        out_specs=pl.BlockSpec((3, 1, NA_KT, 2 * NA_Q), lambda hp: (0, hp, 0, 0)),
        out_shape=jax.ShapeDtypeStruct((3, n_pairs, NA_KT, 2 * NA_Q), jnp.float32),
        compiler_params=pltpu.CompilerParams(dimension_semantics=("parallel",), vmem_limit_bytes=VMEM_LIMIT),
        name="na_bias",
    )(w)


def _mla_kernel(q_ref, k_ref, vt_ref, km_ref, vtm_ref, o_ref, s_ref, p_ref):
    bf = jnp.bfloat16
    f32 = jnp.float32
    n_sub = q_ref.shape[0] // MLA_SUB
    n_chunks = k_ref.shape[0] // MLA_TK
    head = (slice(0, HEAD_GROUP), slice(HEAD_GROUP, 2 * HEAD_GROUP))
    chunk = [slice(c * MLA_TK, (c + 1) * MLA_TK) for c in range(n_chunks)]
    units = [(slice(i * MLA_SUB, (i + 1) * MLA_SUB), hh) for i in range(n_sub) for hh in range(2)]
    n_units = len(units)

    def meta_scores(u):
        rows, hh = units[u]
        return lax.dot_general(km_ref[:, head[hh]], q_ref[rows, head[hh]], _NT, preferred_element_type=f32)

    def scores(u, c):
        rows, hh = units[u]
        s = lax.dot_general(k_ref[chunk[c], head[hh]], q_ref[rows, head[hh]], _NT, preferred_element_type=f32)
        s_ref[u % 2, chunk[c], :] = s
        return _fold_rows(s, jnp.max)

    def probs(u, c, m):
        p_ref[u % 2, chunk[c], :] = jnp.exp2(s_ref[u % 2, chunk[c], :] - m).astype(bf)

    def weighted(u, c):
        hh = units[u][1]
        return jnp.dot(vt_ref[head[hh], chunk[c]], p_ref[u % 2, chunk[c], :], preferred_element_type=f32)

    s_meta, col_max, acc = {}, {}, {}
    for k in range(n_units + 2):
        if k < n_units:
            s_meta[k] = meta_scores(k)
            mx = _fold_rows(s_meta[k], jnp.max)
        if k >= 2:
            p_meta = jnp.exp2(s_meta[k - 2] - col_max[k - 2]).astype(bf)
            acc[k - 2] = jnp.dot(vtm_ref[head[units[k - 2][1]], :], p_meta, preferred_element_type=f32)
        for c in range(n_chunks):
            if k < n_units:
                mx = jnp.maximum(mx, scores(k, c))
            if 1 <= k <= n_units:
                probs(k - 1, c, col_max[k - 1])
            if k >= 2:
                acc[k - 2] = acc[k - 2] + weighted(k - 2, c)
        if k < n_units:
            col_max[k] = jnp.max(mx, axis=0, keepdims=True)

    for i in range(n_sub):
        o0, o1 = [(acc[2 * i + hh] / acc[2 * i + hh][MLA_V_DIM:MLA_V_DIM + 1, :])[:MLA_V_DIM] for hh in range(2)]
        o_ref[units[2 * i][0], :] = jnp.concatenate([o0, o1], axis=0).T.astype(bf)


def _mla_attention(qm, km, vt, km_meta, vt_meta, batch, seq, tq):
    nq = seq // tq
    n_pairs = MLA_HEADS // 2
    pair = 2 * HEAD_GROUP
    return pl.pallas_call(
        _mla_kernel,
        grid=(batch, n_pairs, nq),
        in_specs=[
            pl.BlockSpec((tq, pair), lambda b, hp, i: (b * nq + i, hp)),
            pl.BlockSpec((seq, pair), lambda b, hp, i: (b, hp)),
            pl.BlockSpec((pair, seq), lambda b, hp, i: (hp, b)),
            pl.BlockSpec((N_META, pair), lambda b, hp, i: (0, hp)),
            pl.BlockSpec((pair, N_META), lambda b, hp, i: (hp, 0)),
        ],
        out_specs=pl.BlockSpec((tq, 2 * MLA_V_DIM), lambda b, hp, i: (b * nq + i, hp)),
        out_shape=jax.ShapeDtypeStruct((batch * seq, MLA_HEADS * MLA_V_DIM), jnp.bfloat16),
        scratch_shapes=[pltpu.VMEM((2, seq, MLA_SUB), jnp.float32), pltpu.VMEM((2, seq, MLA_SUB), jnp.bfloat16)],
        compiler_params=pltpu.CompilerParams(
            dimension_semantics=("parallel", "parallel", "arbitrary"), vmem_limit_bytes=VMEM_LIMIT),
        name="mla_attn",
    )(qm, km, vt, km_meta, vt_meta)


def _post_kernel(x_ref, ona_ref, omla_ref, g_ref, wna_ref, wmla_ref, wout_ref, nffn_ref,
                 wff1_ref, wff2_ref, nfin_ref, o_ref):
    bf = jnp.bfloat16
    f32 = jnp.float32
    a = jnp.dot(ona_ref[...], wna_ref[...], preferred_element_type=f32)
    b = jnp.dot(omla_ref[...], wmla_ref[...], preferred_element_type=f32)
    g_na = g_ref[:, :D_MODEL].astype(f32)
    g_mla = g_ref[:, D_MODEL:].astype(f32)
    merged = jax.nn.sigmoid(g_na) * a + jax.nn.sigmoid(g_mla) * b
    h = x_ref[...] + jnp.dot(merged.astype(bf), wout_ref[...], preferred_element_type=f32)
    fn = _rms(h, nffn_ref[...]).astype(bf)
    u = jnp.dot(fn, wff1_ref[...], preferred_element_type=f32)
    u = jnp.square(jnp.maximum(u, 0.0)).astype(bf)
    h = h + jnp.dot(u, wff2_ref[...], preferred_element_type=f32)
    o_ref[...] = _rms(h, nfin_ref[...])


def _post(x2, o_na, o_mla, g, wna, wmla, wout, nffn, wff1, wff2, nfin, tm):
    t = x2.shape[0]
    row = lambda i: (i, 0)
    return pl.pallas_call(
        _post_kernel,
        grid=(t // tm,),
        in_specs=[
            pl.BlockSpec((tm, D_MODEL), row),
            pl.BlockSpec((tm, NA_W), row),
            pl.BlockSpec((tm, MLA_HEADS * MLA_V_DIM), row),
            pl.BlockSpec((tm, 2 * D_MODEL), row),
            _const_spec((NA_W, D_MODEL)),
            _const_spec((MLA_HEADS * MLA_V_DIM, D_MODEL)),
            _const_spec((D_MODEL, D_MODEL)),
            _const_spec((1, D_MODEL)),
            _const_spec((D_MODEL, D_FF)),
            _const_spec((D_FF, D_MODEL)),
            _const_spec((1, D_MODEL)),
        ],
        out_specs=pl.BlockSpec((tm, D_MODEL), row),
        out_shape=jax.ShapeDtypeStruct((t, D_MODEL), jnp.float32),
        compiler_params=pltpu.CompilerParams(
            dimension_semantics=("parallel",), vmem_limit_bytes=VMEM_LIMIT),
        name="post",
    )(x2, o_na, o_mla, g, wna, wmla, wout, nffn, wff1, wff2, nfin)


def _head_groups(w, per_head, lane0):
    k = w.shape[0]
    w = w.reshape(k, MLA_HEADS, per_head)
    w = jnp.pad(w, ((0, 0), (0, 0), (lane0, HEAD_GROUP - lane0 - per_head)))
    return w.reshape(k, MLA_HEADS * HEAD_GROUP)


def _rope_tables(pos0, n, scale):
    pos = np.arange(pos0, pos0 + n, dtype=np.float64)
    inv_freq = 1.0 / (ROPE_THETA ** (np.arange(0, MLA_ROPE_DIM, 2, dtype=np.float64) / MLA_ROPE_DIM))
    ang = pos[:, None] * inv_freq[None, :]
    cos, sin = np.cos(ang), np.sin(ang)
    ones = np.ones((n, ROPE_LANE0))
    tail = HEAD_GROUP - ROPE_LANE0 - MLA_ROPE_DIM
    cos_t = np.concatenate([ones, cos, cos, np.ones((n, tail))], axis=1) * scale
    sin_t = np.concatenate([0 * ones, -sin, sin, np.zeros((n, tail))], axis=1) * scale
    return jnp.asarray(cos_t, jnp.float32), jnp.asarray(sin_t, jnp.float32)


def kernel(x, meta, norm_mix, w_in, na_rpb, mla_q_norm, w_uq, mla_kv_norm, w_ukv, w_na_out, w_mla_out,
           w_out, norm_ffn, w_ff1, w_ff2, norm_final):
    assert norm_mix.shape[0] == 1, "single-layer block"
    batch, seq, d = x.shape
    bf = jnp.bfloat16
    f32 = jnp.float32
    x2 = x.reshape(batch * seq, d)

    wi = w_in[0]
    kr_cols = jnp.pad(wi[:, C_KR:C_KR + MLA_ROPE_DIM],
                      ((0, 0), (ROPE_LANE0, HEAD_GROUP - ROPE_LANE0 - MLA_ROPE_DIM)))
    wqkv = wi[:, :C_CQ].astype(bf)
    wlat = jnp.concatenate([wi[:, C_CQ:C_KR], kr_cols], axis=1).astype(bf)
    wg = wi[:, C_KR + MLA_ROPE_DIM:].astype(bf)
    wuq = _head_groups(w_uq[0], MLA_NOPE_DIM + MLA_ROPE_DIM, 0).astype(bf)
    wukv = w_ukv[0].reshape(MLA_KV_RANK, MLA_HEADS, MLA_NOPE_DIM + MLA_V_DIM)
    wuk = _head_groups(wukv[:, :, :MLA_NOPE_DIM].reshape(MLA_KV_RANK, -1), MLA_NOPE_DIM, 0).astype(bf)
    wuv = _head_groups(wukv[:, :, MLA_NOPE_DIM:].reshape(MLA_KV_RANK, -1), MLA_V_DIM, 0).astype(bf)
    vone = (jnp.arange(MLA_QK) % HEAD_GROUP == MLA_V_DIM).astype(f32)[None]

    mla_scale = (MLA_NOPE_DIM + MLA_ROPE_DIM) ** -0.5 * LOG2E
    cosq, sinq = _rope_tables(N_META, seq, mla_scale)
    cosk, sink = _rope_tables(N_META, seq, 1.0)
    cosq_m, sinq_m = _rope_tables(0, META_PAD, mla_scale)
    cosk_m, sink_m = _rope_tables(0, META_PAD, 1.0)

    small = (norm_mix[0][None], wqkv, wlat, wg, mla_q_norm[0][None], wuq, mla_kv_norm[0][None], wuk, wuv)
    qk, vnat, qm, km, vt, g = _inproj(x2, *small, cosq, sinq, cosk, sink, vone, tm=512)
    meta_pad = jnp.pad(meta.astype(f32), ((0, META_PAD - N_META), (0, 0)))
    qk_meta, vnat_meta, _, km_meta, vt_meta, _ = _inproj(meta_pad, *small, cosq_m, sinq_m, cosk_m, sink_m, vone,
                                                         tm=META_PAD)

    tbl = _na_bias_table(na_rpb[0], seq)
    o_na = _na_attention(qk, vnat, qk_meta[:N_META], vnat_meta[:, :N_META], tbl, batch, seq)
    o_mla = _mla_attention(qm, km, vt, km_meta[:N_META], vt_meta[:, :N_META], batch, seq, tq=4096)

    out = _post(x2, o_na, o_mla, g, w_na_out[0].astype(bf), w_mla_out[0].astype(bf), w_out[0].astype(bf),
                norm_ffn[0][None], w_ff1[0].astype(bf), w_ff2[0].astype(bf), norm_final[None], tm=512)
    return out.reshape(batch, seq, d)
```

```python
import functools

import numpy as np
import jax
import jax.numpy as jnp
from jax import lax
from jax.experimental import pallas as pl
from jax.experimental.pallas import tpu as pltpu

D_MODEL = 1024
N_META = 16
GRID_W = 64
NA_HEADS = 8
NA_HEAD_DIM = 64
NA_WIN_H = 8
NA_WIN_W = 16
NA_W = NA_HEADS * NA_HEAD_DIM
MLA_HEADS = 8
MLA_NOPE_DIM = 64
MLA_ROPE_DIM = 32
MLA_V_DIM = 64
MLA_Q_RANK = 384
MLA_KV_RANK = 256
ROPE_THETA = 10000.0
D_FF = 4 * D_MODEL
EPS = 1e-6

LANES = 128
SUBLANES = 8
HEAD_GROUP = 128
ROPE_LANE0 = MLA_NOPE_DIM
MLA_QK = MLA_HEADS * HEAD_GROUP
NEG = -1e30

C_CQ = 3 * NA_W
C_CKV = C_CQ + MLA_Q_RANK
C_KR = C_CKV + MLA_KV_RANK
C_G = C_KR + HEAD_GROUP

NA_ROWS_PER_BLOCK = 4
NA_KEY_ROWS = NA_ROWS_PER_BLOCK + NA_WIN_H
NA_Q = NA_ROWS_PER_BLOCK * GRID_W
NA_K = NA_KEY_ROWS * GRID_W
NA_KT = NA_K + N_META
NA_TK = 256
NA_BLOCKS_PER_STEP = 16
META_PAD = LANES
LOG2E = float(np.log2(np.e))
MLA_TK = 256
MLA_SUB = 512

VMEM_LIMIT = 56 * 1024 * 1024

_NT = (((1,), (1,)), ((), ()))


def _const_spec(shape):
    nd = len(shape)
    return pl.BlockSpec(shape, lambda *_: (0,) * nd, pipeline_mode=pl.Buffered(1))


def _rms(x, g):
    return x * lax.rsqrt(jnp.mean(x * x, axis=-1, keepdims=True) + EPS) * g


def _rope_group(x, cos, sin):
    lane = lax.broadcasted_iota(jnp.int32, x.shape, 1)
    partner = jnp.where(lane < ROPE_LANE0 + MLA_ROPE_DIM // 2,
                        pltpu.roll(x, HEAD_GROUP - MLA_ROPE_DIM // 2, 1),
                        pltpu.roll(x, MLA_ROPE_DIM // 2, 1))
    return x * cos + partner * sin


def _fold_rows(x, op):
    return op(x.reshape(x.shape[0] // SUBLANES, SUBLANES, x.shape[1]), axis=0)


def _inproj_kernel(x_ref, nmix_ref, wqkv_ref, wlat_ref, wg_ref, qn_ref, wuq_ref, kvn_ref, wuk_ref, wuv_ref,
                   cosq_ref, sinq_ref, cosk_ref, sink_ref, vone_ref,
                   qk_ref, vnat_ref, qm_ref, km_ref, vt_ref, g_ref):
    bf = jnp.bfloat16
    f32 = jnp.float32
    hn = _rms(x_ref[...], nmix_ref[...]).astype(bf)

    lat = jnp.dot(hn, wlat_ref[...], preferred_element_type=f32)
    cq = _rms(lat[:, :MLA_Q_RANK], qn_ref[...]).astype(bf)
    ckv = _rms(lat[:, MLA_Q_RANK:MLA_Q_RANK + MLA_KV_RANK], kvn_ref[...]).astype(bf)
    kr = lat[:, MLA_Q_RANK + MLA_KV_RANK:]
    kr = _rope_group(kr, cosk_ref[...], sink_ref[...])

    qkv = jnp.dot(hn, wqkv_ref[...], preferred_element_type=f32)
    qk_ref[:, :NA_W] = (qkv[:, :NA_W] * (NA_HEAD_DIM ** -0.5 * LOG2E)).astype(bf)
    qk_ref[:, NA_W:] = qkv[:, NA_W:2 * NA_W].astype(bf)
    vnat_ref[...] = qkv[:, 2 * NA_W:].T.astype(bf)

    q = jnp.dot(cq, wuq_ref[...], preferred_element_type=f32)
    kn = jnp.dot(ckv, wuk_ref[...], preferred_element_type=f32)
    vv = jnp.dot(ckv, wuv_ref[...], preferred_element_type=f32)

    g_ref[...] = jnp.dot(hn, wg_ref[...], preferred_element_type=f32).astype(bf)

    cosq = cosq_ref[...]
    sinq = sinq_ref[...]
    for h in range(MLA_HEADS):
        sl = slice(h * HEAD_GROUP, (h + 1) * HEAD_GROUP)
        qm_ref[:, sl] = _rope_group(q[:, sl], cosq, sinq).astype(bf)
        km_ref[:, sl] = (kn[:, sl] + kr).astype(bf)
    vt_ref[...] = (vv + vone_ref[...]).T.astype(bf)


def _inproj(x2, nmix, wqkv, wlat, wg, qn, wuq, kvn, wuk, wuv, cosq, sinq, cosk, sink, vone, tm):
    t = x2.shape[0]
    n_tab = cosq.shape[0] // tm
    row = lambda i: (i, 0)
    col = lambda i: (0, i)
    tab = lambda i: (i % n_tab, 0)
    bf = jnp.bfloat16
    return pl.pallas_call(
        _inproj_kernel,
        grid=(t // tm,),
        in_specs=[
            pl.BlockSpec((tm, D_MODEL), row),
            _const_spec((1, D_MODEL)),
            _const_spec((D_MODEL, C_CQ)),
            _const_spec((D_MODEL, C_G - C_CQ)),
            _const_spec((D_MODEL, 2 * D_MODEL)),
            _const_spec((1, MLA_Q_RANK)),
            _const_spec((MLA_Q_RANK, MLA_QK)),
            _const_spec((1, MLA_KV_RANK)),
            _const_spec((MLA_KV_RANK, MLA_QK)),
            _const_spec((MLA_KV_RANK, MLA_QK)),
            pl.BlockSpec((tm, HEAD_GROUP), tab),
            pl.BlockSpec((tm, HEAD_GROUP), tab),
            pl.BlockSpec((tm, HEAD_GROUP), tab),
            pl.BlockSpec((tm, HEAD_GROUP), tab),
            _const_spec((1, MLA_QK)),
        ],
        out_specs=[
            pl.BlockSpec((tm, 2 * NA_W), row),
            pl.BlockSpec((NA_W, tm), col),
            pl.BlockSpec((tm, MLA_QK), row),
            pl.BlockSpec((tm, MLA_QK), row),
            pl.BlockSpec((MLA_QK, tm), col),
            pl.BlockSpec((tm, 2 * D_MODEL), row),
        ],
        out_shape=[
            jax.ShapeDtypeStruct((t, 2 * NA_W), bf),
            jax.ShapeDtypeStruct((NA_W, t), bf),
            jax.ShapeDtypeStruct((t, MLA_QK), bf),
            jax.ShapeDtypeStruct((t, MLA_QK), bf),
            jax.ShapeDtypeStruct((MLA_QK, t), bf),
            jax.ShapeDtypeStruct((t, 2 * D_MODEL), bf),
        ],
        compiler_params=pltpu.CompilerParams(
            dimension_semantics=("parallel",), vmem_limit_bytes=VMEM_LIMIT),
        name="inproj",
    )(x2, nmix, wqkv, wlat, wg, qn, wuq, kvn, wuk, wuv, cosq, sinq, cosk, sink, vone)


def _na_kernel(q_ref, k_ref, vt_ref, km_ref, vtm_ref, tbl_ref, o_ref, *, n_blocks):
    bf = jnp.bfloat16
    f32 = jnp.float32
    nb = q_ref.shape[0] // NA_Q
    rows = n_blocks * NA_ROWS_PER_BLOCK
    lane = lax.broadcasted_iota(jnp.int32, (NA_Q, LANES), 1)
    in_head = (lane < NA_HEAD_DIM, lane >= NA_HEAD_DIM)

    def window(i):
        j = pl.program_id(2) * nb + i
        ws = jnp.clip(j * NA_ROWS_PER_BLOCK - NA_WIN_H // 2, 0, rows - NA_KEY_ROWS)
        variant = jnp.where(j == 0, 0, jnp.where(j == n_blocks - 1, 2, 1))
        return pl.multiple_of(ws * GRID_W, NA_ROWS_PER_BLOCK * GRID_W), variant

    wins = [window(i) for i in range(nb)]
    s_val, p_val = {}, {}

    def scores(u):
        start, variant = wins[u]
        q2 = q_ref[u * NA_Q:(u + 1) * NA_Q, :]
        zero = jnp.zeros_like(q2)
        qq = jnp.concatenate([jnp.where(in_head[0], q2, zero), jnp.where(in_head[1], q2, zero)], axis=0)
        s_m = lax.dot_general(km_ref[...], qq, _NT, preferred_element_type=f32)
        pieces = [s_m + tbl_ref[variant, 0, NA_K:, :]]
        for r in range(0, NA_K, NA_TK):
            s_g = lax.dot_general(k_ref[pl.ds(start + r, NA_TK), :], qq, _NT, preferred_element_type=f32)
            pieces.append(s_g + tbl_ref[variant, 0, r:r + NA_TK, :])
        s_val[u] = pieces
        mx = _fold_rows(pieces[0], jnp.max)
        for piece in pieces[1:]:
            mx = jnp.maximum(mx, _fold_rows(piece, jnp.max))
        return jnp.max(mx, axis=0, keepdims=True)

    def probs(u, m):
        pieces = [jnp.exp2(piece - m) for piece in s_val.pop(u)]
        p_val[u] = [piece.astype(bf) for piece in pieces]
        total = _fold_rows(pieces[0], jnp.sum)
        for piece in pieces[1:]:
            total = total + _fold_rows(piece, jnp.sum)
        return jnp.sum(total, axis=0, keepdims=True)

    def weighted(u):
        start = wins[u][0]
        pieces = p_val.pop(u)
        acc = jnp.dot(vtm_ref[...], pieces[0], preferred_element_type=f32)
        for i, r in enumerate(range(0, NA_K, NA_TK)):
            acc = acc + jnp.dot(vt_ref[:, pl.ds(start + r, NA_TK)], pieces[i + 1], preferred_element_type=f32)
        return acc

    col_max, denom, out = {}, {}, {}
    for k in range(nb + 2):
        if k < nb:
            col_max[k] = scores(k)
        if 1 <= k <= nb:
            denom[k - 1] = probs(k - 1, col_max[k - 1])
        if k >= 2:
            out[k - 2] = weighted(k - 2) / denom[k - 2]
    for i in range(nb):
        o_t = jnp.concatenate([out[i][:NA_HEAD_DIM, :NA_Q], out[i][NA_HEAD_DIM:, NA_Q:]], axis=0)
        o_ref[i * NA_Q:(i + 1) * NA_Q, :] = o_t.T.astype(bf)


def _na_attention(qk, vt, k_meta, vt_meta, tbl, batch, seq):
    n_blocks = seq // NA_Q
    n_steps = n_blocks // NA_BLOCKS_PER_STEP
    n_pairs = NA_HEADS // 2
    kcol = NA_W // LANES
    tq = NA_BLOCKS_PER_STEP * NA_Q
    return pl.pallas_call(
        functools.partial(_na_kernel, n_blocks=n_blocks),
        grid=(n_pairs, batch, n_steps),
        in_specs=[
            pl.BlockSpec((tq, LANES), lambda hp, b, j: (b * n_steps + j, hp)),
            pl.BlockSpec((seq, LANES), lambda hp, b, j: (b, kcol + hp)),
            pl.BlockSpec((LANES, seq), lambda hp, b, j: (hp, b)),
            pl.BlockSpec((N_META, LANES), lambda hp, b, j: (0, kcol + hp)),
            pl.BlockSpec((LANES, N_META), lambda hp, b, j: (hp, 0)),
            pl.BlockSpec((3, 1, NA_KT, 2 * NA_Q), lambda hp, b, j: (0, hp, 0, 0)),
        ],
        out_specs=pl.BlockSpec((tq, LANES), lambda hp, b, j: (b * n_steps + j, hp)),
        out_shape=jax.ShapeDtypeStruct((batch * seq, NA_W), jnp.bfloat16),
        compiler_params=pltpu.CompilerParams(
            dimension_semantics=("parallel", "parallel", "arbitrary"), vmem_limit_bytes=VMEM_LIMIT),
        name="na_attn",
    )(qk, qk, vt, k_meta, vt_meta, tbl)


def _na_bias_kernel(w_ref, o_ref, *, rows, n_blocks):
    kh = min(NA_WIN_H, rows)
    shape = (GRID_W, LANES)
    lane = lax.broadcasted_iota(jnp.int32, shape, 1)
    kc = lax.broadcasted_iota(jnp.int32, shape, 0)
    qc = lane % GRID_W
    cstart = jnp.clip(qc - NA_WIN_W // 2, 0, GRID_W - NA_WIN_W)
    cvalid = (kc >= cstart) & (kc < cstart + NA_WIN_W)
    neg = jnp.full(shape, NEG, jnp.float32)
    halves = {}

    def toeplitz(hh, dr, half):
        if (hh, dr, half) not in halves:
            row = jnp.broadcast_to(w_ref[hh, dr:dr + 1, :], shape)
            halves[hh, dr, half] = pltpu.roll(row, half * GRID_W, 1, stride=1, stride_axis=0)
        return halves[hh, dr, half]

    for variant, j in enumerate((0, 1, n_blocks - 1)):
        r0 = j * NA_ROWS_PER_BLOCK
        ws = min(max(r0 - NA_WIN_H // 2, 0), rows - NA_KEY_ROWS)
        o_ref[variant, 0, NA_K:, :] = jnp.zeros((N_META, 2 * NA_Q), jnp.float32)
        for hh in range(2):
            for rq_pair in range(NA_ROWS_PER_BLOCK // 2):
                for rk in range(NA_KEY_ROWS):
                    krow = ws + rk
                    tiles = []
                    for half in range(2):
                        r = r0 + 2 * rq_pair + half
                        rs = min(max(r - kh // 2, 0), rows - kh)
                        ok = rs <= krow < rs + kh
                        tiles.append(toeplitz(hh, krow - r + NA_WIN_H - 1, half) if ok else neg)
                    tile = jnp.where(cvalid, jnp.where(lane < GRID_W, tiles[0], tiles[1]), NEG)
                    lane0 = hh * NA_Q + rq_pair * LANES
                    o_ref[variant, 0, rk * GRID_W:(rk + 1) * GRID_W, lane0:lane0 + LANES] = tile


def _na_bias_table(rpb, seq):
    rows = seq // GRID_W
    n_blocks = rows // NA_ROWS_PER_BLOCK
    n_dr = 2 * NA_WIN_H - 1
    n_pairs = NA_HEADS // 2
    rpb = rpb.astype(jnp.float32) * LOG2E
    side = GRID_W - NA_WIN_W
    rpb_pad = jnp.pad(rpb, ((0, 0), (0, 0), (side, side)))
    w = jnp.concatenate([rpb_pad[..., GRID_W - 1:], jnp.zeros(rpb.shape[:2] + (1,), jnp.float32),
                         rpb_pad[..., :GRID_W - 1]], axis=-1)
    w = jnp.roll(jnp.flip(w, axis=-1), 1, axis=-1)
    return pl.pallas_call(
        functools.partial(_na_bias_kernel, rows=rows, n_blocks=n_blocks),
        grid=(n_pairs,),
        in_specs=[pl.BlockSpec((2, n_dr, LANES), lambda hp: (hp, 0, 0))],
        out_specs=pl.BlockSpec((3, 1, NA_KT, 2 * NA_Q), lambda hp: (0, hp, 0, 0)),
        out_shape=jax.ShapeDtypeStruct((3, n_pairs, NA_KT, 2 * NA_Q), jnp.float32),
        compiler_params=pltpu.CompilerParams(dimension_semantics=("parallel",), vmem_limit_bytes=VMEM_LIMIT),
        name="na_bias",
    )(w)


def _mla_kernel(q_ref, k_ref, vt_ref, km_ref, vtm_ref, o_ref, s_ref, p_ref):
    bf = jnp.bfloat16
    f32 = jnp.float32
    n_sub = q_ref.shape[0] // MLA_SUB
    n_chunks = k_ref.shape[0] // MLA_TK
    head = (slice(0, HEAD_GROUP), slice(HEAD_GROUP, 2 * HEAD_GROUP))
    chunk = [slice(c * MLA_TK, (c + 1) * MLA_TK) for c in range(n_chunks)]
    units = [(slice(i * MLA_SUB, (i + 1) * MLA_SUB), hh) for i in range(n_sub) for hh in range(2)]
    n_units = len(units)

    def meta_scores(u):
        rows, hh = units[u]
        return lax.dot_general(km_ref[:, head[hh]], q_ref[rows, head[hh]], _NT, preferred_element_type=f32)

    def scores(u, c):
        rows, hh = units[u]
        s = lax.dot_general(k_ref[chunk[c], head[hh]], q_ref[rows, head[hh]], _NT, preferred_element_type=f32)
        s_ref[u % 2, chunk[c], :] = s
        return _fold_rows(s, jnp.max)

    def probs(u, c, m):
        p_ref[u % 2, chunk[c], :] = jnp.exp2(s_ref[u % 2, chunk[c], :] - m).astype(bf)

    def weighted(u, c):
        hh = units[u][1]
        return jnp.dot(vt_ref[head[hh], chunk[c]], p_ref[u % 2, chunk[c], :], preferred_element_type=f32)

    s_meta, col_max, acc = {}, {}, {}
    for k in range(n_units + 2):
        if k < n_units:
            s_meta[k] = meta_scores(k)
            mx = _fold_rows(s_meta[k], jnp.max)
        if k >= 2:
            p_meta = jnp.exp2(s_meta[k - 2] - col_max[k - 2]).astype(bf)
            acc[k - 2] = jnp.dot(vtm_ref[head[units[k - 2][1]], :], p_meta, preferred_element_type=f32)
        for c in range(n_chunks):
            if k < n_units:
                mx = jnp.maximum(mx, scores(k, c))
            if 1 <= k <= n_units:
                probs(k - 1, c, col_max[k - 1])
            if k >= 2:
                acc[k - 2] = acc[k - 2] + weighted(k - 2, c)
        if k < n_units:
            col_max[k] = jnp.max(mx, axis=0, keepdims=True)

    for i in range(n_sub):
        o0, o1 = [(acc[2 * i + hh] / acc[2 * i + hh][MLA_V_DIM:MLA_V_DIM + 1, :])[:MLA_V_DIM] for hh in range(2)]
        o_ref[units[2 * i][0], :] = jnp.concatenate([o0, o1], axis=0).T.astype(bf)


def _mla_attention(qm, km, vt, km_meta, vt_meta, batch, seq, tq):
    nq = seq // tq
    n_pairs = MLA_HEADS // 2
    pair = 2 * HEAD_GROUP
    return pl.pallas_call(
        _mla_kernel,
        grid=(batch, n_pairs, nq),
        in_specs=[
            pl.BlockSpec((tq, pair), lambda b, hp, i: (b * nq + i, hp)),
            pl.BlockSpec((seq, pair), lambda b, hp, i: (b, hp)),
            pl.BlockSpec((pair, seq), lambda b, hp, i: (hp, b)),
            pl.BlockSpec((N_META, pair), lambda b, hp, i: (0, hp)),
            pl.BlockSpec((pair, N_META), lambda b, hp, i: (hp, 0)),
        ],
        out_specs=pl.BlockSpec((tq, 2 * MLA_V_DIM), lambda b, hp, i: (b * nq + i, hp)),
        out_shape=jax.ShapeDtypeStruct((batch * seq, MLA_HEADS * MLA_V_DIM), jnp.bfloat16),
        scratch_shapes=[pltpu.VMEM((2, seq, MLA_SUB), jnp.float32), pltpu.VMEM((2, seq, MLA_SUB), jnp.bfloat16)],
        compiler_params=pltpu.CompilerParams(
            dimension_semantics=("parallel", "parallel", "arbitrary"), vmem_limit_bytes=VMEM_LIMIT),
        name="mla_attn",
    )(qm, km, vt, km_meta, vt_meta)


def _post_kernel(x_ref, ona_ref, omla_ref, g_ref, wna_ref, wmla_ref, wout_ref, nffn_ref,
                 wff1_ref, wff2_ref, nfin_ref, o_ref):
    bf = jnp.bfloat16
    f32 = jnp.float32
    a = jnp.dot(ona_ref[...], wna_ref[...], preferred_element_type=f32)
    b = jnp.dot(omla_ref[...], wmla_ref[...], preferred_element_type=f32)
    g_na = g_ref[:, :D_MODEL].astype(f32)
    g_mla = g_ref[:, D_MODEL:].astype(f32)
    merged = jax.nn.sigmoid(g_na) * a + jax.nn.sigmoid(g_mla) * b
    h = x_ref[...] + jnp.dot(merged.astype(bf), wout_ref[...], preferred_element_type=f32)
    fn = _rms(h, nffn_ref[...]).astype(bf)
    u = jnp.dot(fn, wff1_ref[...], preferred_element_type=f32)
    u = jnp.square(jnp.maximum(u, 0.0)).astype(bf)
    h = h + jnp.dot(u, wff2_ref[...], preferred_element_type=f32)
    o_ref[...] = _rms(h, nfin_ref[...])


def _post(x2, o_na, o_mla, g, wna, wmla, wout, nffn, wff1, wff2, nfin, tm):
    t = x2.shape[0]
    row = lambda i: (i, 0)
    return pl.pallas_call(
        _post_kernel,
        grid=(t // tm,),
        in_specs=[
            pl.BlockSpec((tm, D_MODEL), row),
            pl.BlockSpec((tm, NA_W), row),
            pl.BlockSpec((tm, MLA_HEADS * MLA_V_DIM), row),
            pl.BlockSpec((tm, 2 * D_MODEL), row),
            _const_spec((NA_W, D_MODEL)),
            _const_spec((MLA_HEADS * MLA_V_DIM, D_MODEL)),
            _const_spec((D_MODEL, D_MODEL)),
            _const_spec((1, D_MODEL)),
            _const_spec((D_MODEL, D_FF)),
            _const_spec((D_FF, D_MODEL)),
            _const_spec((1, D_MODEL)),
        ],
        out_specs=pl.BlockSpec((tm, D_MODEL), row),
        out_shape=jax.ShapeDtypeStruct((t, D_MODEL), jnp.float32),
        compiler_params=pltpu.CompilerParams(
            dimension_semantics=("parallel",), vmem_limit_bytes=VMEM_LIMIT),
        name="post",
    )(x2, o_na, o_mla, g, wna, wmla, wout, nffn, wff1, wff2, nfin)


def _head_groups(w, per_head, lane0):
    k = w.shape[0]
    w = w.reshape(k, MLA_HEADS, per_head)
    w = jnp.pad(w, ((0, 0), (0, 0), (lane0, HEAD_GROUP - lane0 - per_head)))
    return w.reshape(k, MLA_HEADS * HEAD_GROUP)


def _rope_tables(pos0, n, scale):
    pos = np.arange(pos0, pos0 + n, dtype=np.float64)
    inv_freq = 1.0 / (ROPE_THETA ** (np.arange(0, MLA_ROPE_DIM, 2, dtype=np.float64) / MLA_ROPE_DIM))
    ang = pos[:, None] * inv_freq[None, :]
    cos, sin = np.cos(ang), np.sin(ang)
    ones = np.ones((n, ROPE_LANE0))
    tail = HEAD_GROUP - ROPE_LANE0 - MLA_ROPE_DIM
    cos_t = np.concatenate([ones, cos, cos, np.ones((n, tail))], axis=1) * scale
    sin_t = np.concatenate([0 * ones, -sin, sin, np.zeros((n, tail))], axis=1) * scale
    return jnp.asarray(cos_t, jnp.float32), jnp.asarray(sin_t, jnp.float32)


def kernel(x, meta, norm_mix, w_in, na_rpb, mla_q_norm, w_uq, mla_kv_norm, w_ukv, w_na_out, w_mla_out,
           w_out, norm_ffn, w_ff1, w_ff2, norm_final):
    assert norm_mix.shape[0] == 1, "single-layer block"
    batch, seq, d = x.shape
    bf = jnp.bfloat16
    f32 = jnp.float32
    x2 = x.reshape(batch * seq, d)

    wi = w_in[0]
    kr_cols = jnp.pad(wi[:, C_KR:C_KR + MLA_ROPE_DIM],
                      ((0, 0), (ROPE_LANE0, HEAD_GROUP - ROPE_LANE0 - MLA_ROPE_DIM)))
    wqkv = wi[:, :C_CQ].astype(bf)
    wlat = jnp.concatenate([wi[:, C_CQ:C_KR], kr_cols], axis=1).astype(bf)
    wg = wi[:, C_KR + MLA_ROPE_DIM:].astype(bf)
    wuq = _head_groups(w_uq[0], MLA_NOPE_DIM + MLA_ROPE_DIM, 0).astype(bf)
    wukv = w_ukv[0].reshape(MLA_KV_RANK, MLA_HEADS, MLA_NOPE_DIM + MLA_V_DIM)
    wuk = _head_groups(wukv[:, :, :MLA_NOPE_DIM].reshape(MLA_KV_RANK, -1), MLA_NOPE_DIM, 0).astype(bf)
    wuv = _head_groups(wukv[:, :, MLA_NOPE_DIM:].reshape(MLA_KV_RANK, -1), MLA_V_DIM, 0).astype(bf)
    vone = (jnp.arange(MLA_QK) % HEAD_GROUP == MLA_V_DIM).astype(f32)[None]

    mla_scale = (MLA_NOPE_DIM + MLA_ROPE_DIM) ** -0.5 * LOG2E
    cosq, sinq = _rope_tables(N_META, seq, mla_scale)
    cosk, sink = _rope_tables(N_META, seq, 1.0)
    cosq_m, sinq_m = _rope_tables(0, META_PAD, mla_scale)
    cosk_m, sink_m = _rope_tables(0, META_PAD, 1.0)

    small = (norm_mix[0][None], wqkv, wlat, wg, mla_q_norm[0][None], wuq, mla_kv_norm[0][None], wuk, wuv)
    qk, vnat, qm, km, vt, g = _inproj(x2, *small, cosq, sinq, cosk, sink, vone, tm=512)
    meta_pad = jnp.pad(meta.astype(f32), ((0, META_PAD - N_META), (0, 0)))
    qk_meta, vnat_meta, _, km_meta, vt_meta, _ = _inproj(meta_pad, *small, cosq_m, sinq_m, cosk_m, sink_m, vone,
                                                         tm=META_PAD)

    tbl = _na_bias_table(na_rpb[0], seq)
    o_na = _na_attention(qk, vnat, qk_meta[:N_META], vnat_meta[:, :N_META], tbl, batch, seq)
    o_mla = _mla_attention(qm, km, vt, km_meta[:N_META], vt_meta[:, :N_META], batch, seq, tq=4096)

    out = _post(x2, o_na, o_mla, g, w_na_out[0].astype(bf), w_mla_out[0].astype(bf), w_out[0].astype(bf),
                norm_ffn[0][None], w_ff1[0].astype(bf), w_ff2[0].astype(bf), norm_final[None], tm=512)
    return out.reshape(batch, seq, d)
```

```python
import functools

import numpy as np
import jax
import jax.numpy as jnp
from jax import lax
from jax.experimental import pallas as pl
from jax.experimental.pallas import tpu as pltpu

D_MODEL = 1024
N_META = 16
GRID_W = 64
NA_HEADS = 8
NA_HEAD_DIM = 64
NA_WIN_H = 8
NA_WIN_W = 16
NA_W = NA_HEADS * NA_HEAD_DIM
MLA_HEADS = 8
MLA_NOPE_DIM = 64
MLA_ROPE_DIM = 32
MLA_V_DIM = 64
MLA_Q_RANK = 384
MLA_KV_RANK = 256
ROPE_THETA = 10000.0
D_FF = 4 * D_MODEL
EPS = 1e-6

LANES = 128
SUBLANES = 8
HEAD_GROUP = 128
ROPE_LANE0 = MLA_NOPE_DIM
MLA_QK = MLA_HEADS * HEAD_GROUP
NEG = -1e30

C_CQ = 3 * NA_W
C_CKV = C_CQ + MLA_Q_RANK
C_KR = C_CKV + MLA_KV_RANK
C_G = C_KR + HEAD_GROUP

NA_ROWS_PER_BLOCK = 4
NA_KEY_ROWS = NA_ROWS_PER_BLOCK + NA_WIN_H
NA_Q = NA_ROWS_PER_BLOCK * GRID_W
NA_K = NA_KEY_ROWS * GRID_W
NA_KT = NA_K + N_META
NA_BLOCKS_PER_STEP = 16
META_PAD = LANES
LOG2E = float(np.log2(np.e))
MLA_TK = 256
MLA_SUB = 512

VMEM_LIMIT = 56 * 1024 * 1024

_NT = (((1,), (1,)), ((), ()))


def _const_spec(shape):
    nd = len(shape)
    return pl.BlockSpec(shape, lambda *_: (0,) * nd, pipeline_mode=pl.Buffered(1))


def _rms(x, g):
    return x * lax.rsqrt(jnp.mean(x * x, axis=-1, keepdims=True) + EPS) * g


def _rope_group(x, cos, sin):
    lane = lax.broadcasted_iota(jnp.int32, x.shape, 1)
    partner = jnp.where(lane < ROPE_LANE0 + MLA_ROPE_DIM // 2,
                        pltpu.roll(x, HEAD_GROUP - MLA_ROPE_DIM // 2, 1),
                        pltpu.roll(x, MLA_ROPE_DIM // 2, 1))
    return x * cos + partner * sin


def _fold_rows(x, op):
    return op(x.reshape(x.shape[0] // SUBLANES, SUBLANES, x.shape[1]), axis=0)


def _inproj_kernel(x_ref, nmix_ref, wqkv_ref, wlat_ref, wg_ref, qn_ref, wuq_ref, kvn_ref, wuk_ref, wuv_ref,
                   cosq_ref, sinq_ref, cosk_ref, sink_ref, vone_ref,
                   qk_ref, vnat_ref, qm_ref, km_ref, vt_ref, g_ref):
    bf = jnp.bfloat16
    f32 = jnp.float32
    hn = _rms(x_ref[...], nmix_ref[...]).astype(bf)

    lat = jnp.dot(hn, wlat_ref[...], preferred_element_type=f32)
    cq = _rms(lat[:, :MLA_Q_RANK], qn_ref[...]).astype(bf)
    ckv = _rms(lat[:, MLA_Q_RANK:MLA_Q_RANK + MLA_KV_RANK], kvn_ref[...]).astype(bf)
    kr = lat[:, MLA_Q_RANK + MLA_KV_RANK:]
    kr = _rope_group(kr, cosk_ref[...], sink_ref[...])

    qkv = jnp.dot(hn, wqkv_ref[...], preferred_element_type=f32)
    qk_ref[:, :NA_W] = (qkv[:, :NA_W] * (NA_HEAD_DIM ** -0.5 * LOG2E)).astype(bf)
    qk_ref[:, NA_W:] = qkv[:, NA_W:2 * NA_W].astype(bf)
    vnat_ref[...] = qkv[:, 2 * NA_W:].T.astype(bf)

    q = jnp.dot(cq, wuq_ref[...], preferred_element_type=f32)
    kn = jnp.dot(ckv, wuk_ref[...], preferred_element_type=f32)
    vv = jnp.dot(ckv, wuv_ref[...], preferred_element_type=f32)

    g_ref[...] = jnp.dot(hn, wg_ref[...], preferred_element_type=f32).astype(bf)

    cosq = cosq_ref[...]
    sinq = sinq_ref[...]
    for h in range(MLA_HEADS):
        sl = slice(h * HEAD_GROUP, (h + 1) * HEAD_GROUP)
        qm_ref[:, sl] = _rope_group(q[:, sl], cosq, sinq).astype(bf)
        km_ref[:, sl] = (kn[:, sl] + kr).astype(bf)
    vt_ref[...] = (vv + vone_ref[...]).T.astype(bf)


def _inproj(x2, nmix, wqkv, wlat, wg, qn, wuq, kvn, wuk, wuv, cosq, sinq, cosk, sink, vone, tm):
    t = x2.shape[0]
    n_tab = cosq.shape[0] // tm
    row = lambda i: (i, 0)
    col = lambda i: (0, i)
    tab = lambda i: (i % n_tab, 0)
    bf = jnp.bfloat16
    return pl.pallas_call(
        _inproj_kernel,
        grid=(t // tm,),
        in_specs=[
            pl.BlockSpec((tm, D_MODEL), row),
            _const_spec((1, D_MODEL)),
            _const_spec((D_MODEL, C_CQ)),
            _const_spec((D_MODEL, C_G - C_CQ)),
            _const_spec((D_MODEL, 2 * D_MODEL)),
            _const_spec((1, MLA_Q_RANK)),
            _const_spec((MLA_Q_RANK, MLA_QK)),
            _const_spec((1, MLA_KV_RANK)),
            _const_spec((MLA_KV_RANK, MLA_QK)),
            _const_spec((MLA_KV_RANK, MLA_QK)),
            pl.BlockSpec((tm, HEAD_GROUP), tab),
            pl.BlockSpec((tm, HEAD_GROUP), tab),
            pl.BlockSpec((tm, HEAD_GROUP), tab),
            pl.BlockSpec((tm, HEAD_GROUP), tab),
            _const_spec((1, MLA_QK)),
        ],
        out_specs=[
            pl.BlockSpec((tm, 2 * NA_W), row),
            pl.BlockSpec((NA_W, tm), col),
            pl.BlockSpec((tm, MLA_QK), row),
            pl.BlockSpec((tm, MLA_QK), row),
            pl.BlockSpec((MLA_QK, tm), col),
            pl.BlockSpec((tm, 2 * D_MODEL), row),
        ],
        out_shape=[
            jax.ShapeDtypeStruct((t, 2 * NA_W), bf),
            jax.ShapeDtypeStruct((NA_W, t), bf),
            jax.ShapeDtypeStruct((t, MLA_QK), bf),
            jax.ShapeDtypeStruct((t, MLA_QK), bf),
            jax.ShapeDtypeStruct((MLA_QK, t), bf),
            jax.ShapeDtypeStruct((t, 2 * D_MODEL), bf),
        ],
        compiler_params=pltpu.CompilerParams(
            dimension_semantics=("parallel",), vmem_limit_bytes=VMEM_LIMIT,
            allow_input_fusion=[False, False, True, True, True, False, True, False, True, True,
                                False, False, False, False, False]),
        name="inproj",
    )(x2, nmix, wqkv, wlat, wg, qn, wuq, kvn, wuk, wuv, cosq, sinq, cosk, sink, vone)


def _na_kernel(q_ref, k_ref, vt_ref, km_ref, vtm_ref, tbl_ref, o_ref, *, n_blocks):
    bf = jnp.bfloat16
    f32 = jnp.float32
    nb = q_ref.shape[0] // NA_Q
    rows = n_blocks * NA_ROWS_PER_BLOCK
    lane = lax.broadcasted_iota(jnp.int32, (NA_Q, LANES), 1)
    in_head = (lane < NA_HEAD_DIM, lane >= NA_HEAD_DIM)

    def window(i):
        j = pl.program_id(2) * nb + i
        ws = jnp.clip(j * NA_ROWS_PER_BLOCK - NA_WIN_H // 2, 0, rows - NA_KEY_ROWS)
        variant = jnp.where(j == 0, 0, jnp.where(j == n_blocks - 1, 2, 1))
        return pl.multiple_of(ws * GRID_W, NA_ROWS_PER_BLOCK * GRID_W), variant

    wins = [window(i) for i in range(nb)]
    s_val, p_val = {}, {}

    def scores(u):
        start, variant = wins[u]
        q2 = q_ref[u * NA_Q:(u + 1) * NA_Q, :]
        zero = jnp.zeros_like(q2)
        qq = jnp.concatenate([jnp.where(in_head[0], q2, zero), jnp.where(in_head[1], q2, zero)], axis=0)
        s_g = lax.dot_general(k_ref[pl.ds(start, NA_K), :], qq, _NT, preferred_element_type=f32)
        s_g = s_g + tbl_ref[variant, 0, :NA_K, :]
        s_m = lax.dot_general(km_ref[...], qq, _NT, preferred_element_type=f32)
        s_m = s_m + tbl_ref[variant, 0, NA_K:, :]
        s_val[u] = (s_g, s_m)
        mx = jnp.maximum(_fold_rows(s_g, jnp.max), _fold_rows(s_m, jnp.max))
        return jnp.max(mx, axis=0, keepdims=True)

    def probs(u, m):
        s_g, s_m = s_val.pop(u)
        p_g = jnp.exp2(s_g - m)
        p_m = jnp.exp2(s_m - m)
        p_val[u] = (p_g.astype(bf), p_m.astype(bf))
        return jnp.sum(_fold_rows(p_g, jnp.sum) + _fold_rows(p_m, jnp.sum), axis=0, keepdims=True)

    def weighted(u):
        start = wins[u][0]
        p_g, p_m = p_val.pop(u)
        return (jnp.dot(vt_ref[:, pl.ds(start, NA_K)], p_g, preferred_element_type=f32)
                + jnp.dot(vtm_ref[...], p_m, preferred_element_type=f32))

    col_max, denom, out = {}, {}, {}
    for k in range(nb + 2):
        if k < nb:
            col_max[k] = scores(k)
        if 1 <= k <= nb:
            denom[k - 1] = probs(k - 1, col_max[k - 1])
        if k >= 2:
            out[k - 2] = weighted(k - 2) / denom[k - 2]
    for i in range(nb):
        o_t = jnp.concatenate([out[i][:NA_HEAD_DIM, :NA_Q], out[i][NA_HEAD_DIM:, NA_Q:]], axis=0)
        o_ref[i * NA_Q:(i + 1) * NA_Q, :] = o_t.T.astype(bf)


def _na_attention(qk, vt, k_meta, vt_meta, tbl, batch, seq):
    n_blocks = seq // NA_Q
    n_steps = n_blocks // NA_BLOCKS_PER_STEP
    n_pairs = NA_HEADS // 2
    kcol = NA_W // LANES
    tq = NA_BLOCKS_PER_STEP * NA_Q
    return pl.pallas_call(
        functools.partial(_na_kernel, n_blocks=n_blocks),
        grid=(n_pairs, batch, n_steps),
        in_specs=[
            pl.BlockSpec((tq, LANES), lambda hp, b, j: (b * n_steps + j, hp)),
            pl.BlockSpec((seq, LANES), lambda hp, b, j: (b, kcol + hp)),
            pl.BlockSpec((LANES, seq), lambda hp, b, j: (hp, b)),
            pl.BlockSpec((N_META, LANES), lambda hp, b, j: (0, kcol + hp)),
            pl.BlockSpec((LANES, N_META), lambda hp, b, j: (hp, 0)),
            pl.BlockSpec((3, 1, NA_KT, 2 * NA_Q), lambda hp, b, j: (0, hp, 0, 0)),
        ],
        out_specs=pl.BlockSpec((tq, LANES), lambda hp, b, j: (b * n_steps + j, hp)),
        out_shape=jax.ShapeDtypeStruct((batch * seq, NA_W), jnp.bfloat16),
        compiler_params=pltpu.CompilerParams(
            dimension_semantics=("parallel", "parallel", "arbitrary"), vmem_limit_bytes=VMEM_LIMIT),
        name="na_attn",
    )(qk, qk, vt, k_meta, vt_meta, tbl)


def _na_bias_kernel(w_ref, o_ref, *, rows, n_blocks):
    kh = min(NA_WIN_H, rows)
    shape = (GRID_W, LANES)
    lane = lax.broadcasted_iota(jnp.int32, shape, 1)
    kc = lax.broadcasted_iota(jnp.int32, shape, 0)
    qc = lane % GRID_W
    cstart = jnp.clip(qc - NA_WIN_W // 2, 0, GRID_W - NA_WIN_W)
    cvalid = (kc >= cstart) & (kc < cstart + NA_WIN_W)
    neg = jnp.full(shape, NEG, jnp.float32)
    halves = {}

    def toeplitz(hh, dr, half):
        if (hh, dr, half) not in halves:
            row = jnp.broadcast_to(w_ref[hh, dr:dr + 1, :], shape)
            halves[hh, dr, half] = pltpu.roll(row, half * GRID_W, 1, stride=1, stride_axis=0)
        return halves[hh, dr, half]

    for variant, j in enumerate((0, 1, n_blocks - 1)):
        r0 = j * NA_ROWS_PER_BLOCK
        ws = min(max(r0 - NA_WIN_H // 2, 0), rows - NA_KEY_ROWS)
        o_ref[variant, 0, NA_K:, :] = jnp.zeros((N_META, 2 * NA_Q), jnp.float32)
        for hh in range(2):
            for rq_pair in range(NA_ROWS_PER_BLOCK // 2):
                for rk in range(NA_KEY_ROWS):
                    krow = ws + rk
                    tiles = []
                    for half in range(2):
                        r = r0 + 2 * rq_pair + half
                        rs = min(max(r - kh // 2, 0), rows - kh)
                        ok = rs <= krow < rs + kh
                        tiles.append(toeplitz(hh, krow - r + NA_WIN_H - 1, half) if ok else neg)
                    tile = jnp.where(cvalid, jnp.where(lane < GRID_W, tiles[0], tiles[1]), NEG)
                    lane0 = hh * NA_Q + rq_pair * LANES
                    o_ref[variant, 0, rk * GRID_W:(rk + 1) * GRID_W, lane0:lane0 + LANES] = tile


def _na_bias_table(rpb, seq):
    rows = seq // GRID_W
    n_blocks = rows // NA_ROWS_PER_BLOCK
    n_dr = 2 * NA_WIN_H - 1
    n_pairs = NA_HEADS // 2
    rpb = rpb.astype(jnp.float32) * LOG2E
    side = GRID_W - NA_WIN_W
    rpb_pad = jnp.pad(rpb, ((0, 0), (0, 0), (side, side)))
    w = jnp.concatenate([rpb_pad[..., GRID_W - 1:], jnp.zeros(rpb.shape[:2] + (1,), jnp.float32),
                         rpb_pad[..., :GRID_W - 1]], axis=-1)
    w = jnp.roll(jnp.flip(w, axis=-1), 1, axis=-1)
    return pl.pallas_call(
        functools.partial(_na_bias_kernel, rows=rows, n_blocks=n_blocks),
        grid=(n_pairs,),
        in_specs=[pl.BlockSpec((2, n_dr, LANES), lambda hp: (hp, 0, 0))],
        out_specs=pl.BlockSpec((3, 1, NA_KT, 2 * NA_Q), lambda hp: (0, hp, 0, 0)),
        out_shape=jax.ShapeDtypeStruct((3, n_pairs, NA_KT, 2 * NA_Q), jnp.float32),
        compiler_params=pltpu.CompilerParams(dimension_semantics=("parallel",), vmem_limit_bytes=VMEM_LIMIT),
        name="na_bias",
    )(w)


def _mla_kernel(q_ref, k_ref, vt_ref, km_ref, vtm_ref, o_ref, s_ref, p_ref):
    bf = jnp.bfloat16
    f32 = jnp.float32
    n_sub = q_ref.shape[0] // MLA_SUB
    n_chunks = k_ref.shape[0] // MLA_TK
    head = (slice(0, HEAD_GROUP), slice(HEAD_GROUP, 2 * HEAD_GROUP))
    chunk = [slice(c * MLA_TK, (c + 1) * MLA_TK) for c in range(n_chunks)]
    units = [(slice(i * MLA_SUB, (i + 1) * MLA_SUB), hh) for i in range(n_sub) for hh in range(2)]
    n_units = len(units)

    def meta_scores(u):
        rows, hh = units[u]
        return lax.dot_general(km_ref[:, head[hh]], q_ref[rows, head[hh]], _NT, preferred_element_type=f32)

    def scores(u, c):
        rows, hh = units[u]
        s = lax.dot_general(k_ref[chunk[c], head[hh]], q_ref[rows, head[hh]], _NT, preferred_element_type=f32)
        s_ref[u % 2, chunk[c], :] = s
        return _fold_rows(s, jnp.max)

    def probs(u, c, m):
        p_ref[u % 2, chunk[c], :] = jnp.exp2(s_ref[u % 2, chunk[c], :] - m).astype(bf)

    def weighted(u, c):
        hh = units[u][1]
        return jnp.dot(vt_ref[head[hh], chunk[c]], p_ref[u % 2, chunk[c], :], preferred_element_type=f32)

    s_meta, col_max, acc = {}, {}, {}
    for k in range(n_units + 2):
        if k < n_units:
            s_meta[k] = meta_scores(k)
            mx = _fold_rows(s_meta[k], jnp.max)
        if k >= 2:
            p_meta = jnp.exp2(s_meta[k - 2] - col_max[k - 2]).astype(bf)
            acc[k - 2] = jnp.dot(vtm_ref[head[units[k - 2][1]], :], p_meta, preferred_element_type=f32)
        for c in range(n_chunks):
            if k < n_units:
                mx = jnp.maximum(mx, scores(k, c))
            if 1 <= k <= n_units:
                probs(k - 1, c, col_max[k - 1])
            if k >= 2:
                acc[k - 2] = acc[k - 2] + weighted(k - 2, c)
        if k < n_units:
            col_max[k] = jnp.max(mx, axis=0, keepdims=True)

    for i in range(n_sub):
        o0, o1 = [(acc[2 * i + hh] / acc[2 * i + hh][MLA_V_DIM:MLA_V_DIM + 1, :])[:MLA_V_DIM] for hh in range(2)]
        o_ref[units[2 * i][0], :] = jnp.concatenate([o0, o1], axis=0).T.astype(bf)


def _mla_attention(qm, km, vt, km_meta, vt_meta, batch, seq, tq):
    nq = seq // tq
    n_pairs = MLA_HEADS // 2
    pair = 2 * HEAD_GROUP
    return pl.pallas_call(
        _mla_kernel,
        grid=(batch, n_pairs, nq),
        in_specs=[
            pl.BlockSpec((tq, pair), lambda b, hp, i: (b * nq + i, hp)),
            pl.BlockSpec((seq, pair), lambda b, hp, i: (b, hp)),
            pl.BlockSpec((pair, seq), lambda b, hp, i: (hp, b)),
            pl.BlockSpec((N_META, pair), lambda b, hp, i: (0, hp)),
            pl.BlockSpec((pair, N_META), lambda b, hp, i: (hp, 0)),
        ],
        out_specs=pl.BlockSpec((tq, 2 * MLA_V_DIM), lambda b, hp, i: (b * nq + i, hp)),
        out_shape=jax.ShapeDtypeStruct((batch * seq, MLA_HEADS * MLA_V_DIM), jnp.bfloat16),
        scratch_shapes=[pltpu.VMEM((2, seq, MLA_SUB), jnp.float32), pltpu.VMEM((2, seq, MLA_SUB), jnp.bfloat16)],
        compiler_params=pltpu.CompilerParams(
            dimension_semantics=("parallel", "parallel", "arbitrary"), vmem_limit_bytes=VMEM_LIMIT),
        name="mla_attn",
    )(qm, km, vt, km_meta, vt_meta)


def _post_kernel(x_ref, ona_ref, omla_ref, g_ref, wna_ref, wmla_ref, wout_ref, nffn_ref,
                 wff1_ref, wff2_ref, nfin_ref, o_ref):
    bf = jnp.bfloat16
    f32 = jnp.float32
    a = jnp.dot(ona_ref[...], wna_ref[...], preferred_element_type=f32)
    b = jnp.dot(omla_ref[...], wmla_ref[...], preferred_element_type=f32)
    g_na = g_ref[:, :D_MODEL].astype(f32)
    g_mla = g_ref[:, D_MODEL:].astype(f32)
    merged = jax.nn.sigmoid(g_na) * a + jax.nn.sigmoid(g_mla) * b
    h = x_ref[...] + jnp.dot(merged.astype(bf), wout_ref[...], preferred_element_type=f32)
    fn = _rms(h, nffn_ref[...]).astype(bf)
    u = jnp.dot(fn, wff1_ref[...], preferred_element_type=f32)
    u = jnp.square(jnp.maximum(u, 0.0)).astype(bf)
    h = h + jnp.dot(u, wff2_ref[...], preferred_element_type=f32)
    o_ref[...] = _rms(h, nfin_ref[...])


def _post(x2, o_na, o_mla, g, wna, wmla, wout, nffn, wff1, wff2, nfin, tm):
    t = x2.shape[0]
    row = lambda i: (i, 0)
    return pl.pallas_call(
        _post_kernel,
        grid=(t // tm,),
        in_specs=[
            pl.BlockSpec((tm, D_MODEL), row),
            pl.BlockSpec((tm, NA_W), row),
            pl.BlockSpec((tm, MLA_HEADS * MLA_V_DIM), row),
            pl.BlockSpec((tm, 2 * D_MODEL), row),
            _const_spec((NA_W, D_MODEL)),
            _const_spec((MLA_HEADS * MLA_V_DIM, D_MODEL)),
            _const_spec((D_MODEL, D_MODEL)),
            _const_spec((1, D_MODEL)),
            _const_spec((D_MODEL, D_FF)),
            _const_spec((D_FF, D_MODEL)),
            _const_spec((1, D_MODEL)),
        ],
        out_specs=pl.BlockSpec((tm, D_MODEL), row),
        out_shape=jax.ShapeDtypeStruct((t, D_MODEL), jnp.float32),
        compiler_params=pltpu.CompilerParams(
            dimension_semantics=("parallel",), vmem_limit_bytes=VMEM_LIMIT),
        name="post",
    )(x2, o_na, o_mla, g, wna, wmla, wout, nffn, wff1, wff2, nfin)


def _head_groups(w, per_head, lane0):
    k = w.shape[0]
    w = w.reshape(k, MLA_HEADS, per_head)
    w = jnp.pad(w, ((0, 0), (0, 0), (lane0, HEAD_GROUP - lane0 - per_head)))
    return w.reshape(k, MLA_HEADS * HEAD_GROUP)


def _rope_tables(pos0, n, scale):
    pos = np.arange(pos0, pos0 + n, dtype=np.float64)
    inv_freq = 1.0 / (ROPE_THETA ** (np.arange(0, MLA_ROPE_DIM, 2, dtype=np.float64) / MLA_ROPE_DIM))
    ang = pos[:, None] * inv_freq[None, :]
    cos, sin = np.cos(ang), np.sin(ang)
    ones = np.ones((n, ROPE_LANE0))
    tail = HEAD_GROUP - ROPE_LANE0 - MLA_ROPE_DIM
    cos_t = np.concatenate([ones, cos, cos, np.ones((n, tail))], axis=1) * scale
    sin_t = np.concatenate([0 * ones, -sin, sin, np.zeros((n, tail))], axis=1) * scale
    return jnp.asarray(cos_t, jnp.float32), jnp.asarray(sin_t, jnp.float32)


def kernel(x, meta, norm_mix, w_in, na_rpb, mla_q_norm, w_uq, mla_kv_norm, w_ukv, w_na_out, w_mla_out,
           w_out, norm_ffn, w_ff1, w_ff2, norm_final):
    assert norm_mix.shape[0] == 1, "single-layer block"
    batch, seq, d = x.shape
    bf = jnp.bfloat16
    f32 = jnp.float32
    x2 = x.reshape(batch * seq, d)

    wi = w_in[0]
    kr_cols = jnp.pad(wi[:, C_KR:C_KR + MLA_ROPE_DIM],
                      ((0, 0), (ROPE_LANE0, HEAD_GROUP - ROPE_LANE0 - MLA_ROPE_DIM)))
    wqkv = wi[:, :C_CQ].astype(bf)
    wlat = jnp.concatenate([wi[:, C_CQ:C_KR], kr_cols], axis=1).astype(bf)
    wg = wi[:, C_KR + MLA_ROPE_DIM:].astype(bf)
    wuq = _head_groups(w_uq[0], MLA_NOPE_DIM + MLA_ROPE_DIM, 0).astype(bf)
    wukv = w_ukv[0].reshape(MLA_KV_RANK, MLA_HEADS, MLA_NOPE_DIM + MLA_V_DIM)
    wuk = _head_groups(wukv[:, :, :MLA_NOPE_DIM].reshape(MLA_KV_RANK, -1), MLA_NOPE_DIM, 0).astype(bf)
    wuv = _head_groups(wukv[:, :, MLA_NOPE_DIM:].reshape(MLA_KV_RANK, -1), MLA_V_DIM, 0).astype(bf)
    vone = (jnp.arange(MLA_QK) % HEAD_GROUP == MLA_V_DIM).astype(f32)[None]

    mla_scale = (MLA_NOPE_DIM + MLA_ROPE_DIM) ** -0.5 * LOG2E
    cosq, sinq = _rope_tables(N_META, seq, mla_scale)
    cosk, sink = _rope_tables(N_META, seq, 1.0)
    cosq_m, sinq_m = _rope_tables(0, META_PAD, mla_scale)
    cosk_m, sink_m = _rope_tables(0, META_PAD, 1.0)

    small = (norm_mix[0][None], wqkv, wlat, wg, mla_q_norm[0][None], wuq, mla_kv_norm[0][None], wuk, wuv)
    qk, vnat, qm, km, vt, g = _inproj(x2, *small, cosq, sinq, cosk, sink, vone, tm=512)
    meta_pad = jnp.pad(meta.astype(f32), ((0, META_PAD - N_META), (0, 0)))
    qk_meta, vnat_meta, _, km_meta, vt_meta, _ = _inproj(meta_pad, *small, cosq_m, sinq_m, cosk_m, sink_m, vone,
                                                         tm=META_PAD)

    tbl = _na_bias_table(na_rpb[0], seq)
    o_na = _na_attention(qk, vnat, qk_meta[:N_META], vnat_meta[:, :N_META], tbl, batch, seq)
    o_mla = _mla_attention(qm, km, vt, km_meta[:N_META], vt_meta[:, :N_META], batch, seq, tq=4096)

    out = _post(x2, o_na, o_mla, g, w_na_out[0].astype(bf), w_mla_out[0].astype(bf), w_out[0].astype(bf),
                norm_ffn[0][None], w_ff1[0].astype(bf), w_ff2[0].astype(bf), norm_final[None], tm=512)
    return out.reshape(batch, seq, d)
```
